```python
import jax, jax.numpy as jnp
from jax import lax
import numpy as np

D_MODEL = 2048
BATCH = 32
SEQ = 256
DEPTH = 1
DEC_BATCH = 4
DEC_SEQ = 1024
PAST_LEN = 256

GRID_W = 64
NA_HEADS = 16
NA_HEAD_DIM = D_MODEL // NA_HEADS
NA_WIDTH = NA_HEADS * NA_HEAD_DIM
NA_WIN_ROWS = 8
NA_WIN_COLS = 16
Q_BLOCK = 128
SSD_D_INNER = 2 * D_MODEL
SSD_HEAD_DIM = 64
SSD_HEADS = SSD_D_INNER // SSD_HEAD_DIM
SSD_GROUPS = 8
SSD_STATE = 128
SSD_CONV = 5
SSD_CHUNK = 128
SSD_CONV_CH = SSD_D_INNER + 2 * SSD_GROUPS * SSD_STATE
PEER_HEADS = 8
PEER_KEY_DIM = 256
PEER_N_KEYS = 128
PEER_N_EXPERTS = PEER_N_KEYS * PEER_N_KEYS
PEER_TOPK = 16
PEER_TOKEN_BLOCK = 128
N_BRANCHES = 2
IN_SPLITS = (NA_WIDTH, 2 * NA_WIDTH, 3 * NA_WIDTH,
             3 * NA_WIDTH + SSD_D_INNER,
             3 * NA_WIDTH + SSD_D_INNER + SSD_CONV_CH,
             3 * NA_WIDTH + SSD_D_INNER + SSD_CONV_CH + 2 * SSD_HEADS)
IN_COLS = 3 * NA_WIDTH + SSD_D_INNER + SSD_CONV_CH + 2 * SSD_HEADS + N_BRANCHES * D_MODEL
RMS_EPS = 1e-6

kernel_name = "hybrid_na_ssd_peer_diffusion_step"


def rmsnorm(x, g):
    xf = x.astype(jnp.float32)
    y = xf * lax.rsqrt(jnp.mean(xf * xf, axis=-1, keepdims=True) + RMS_EPS)
    return (y * g.astype(jnp.float32)).astype(x.dtype)


def dwconv_centred(x, w, b):
    pad = SSD_CONV // 2
    y = lax.conv_general_dilated(x, w[:, None, :].astype(x.dtype), window_strides=(1,),
                                 padding=[(pad, pad)], dimension_numbers=('NWC', 'WIO', 'NWC'),
                                 feature_group_count=x.shape[-1])
    return y + b


def ssd_scan(x, dt, a, bm, cm, init):
    f32 = jnp.float32
    b, l, h, p = x.shape
    g, n = bm.shape[2], bm.shape[3]
    r = h // g
    nc = l // SSD_CHUNK
    xdt = (x.astype(f32) * dt[..., None]).reshape(b, nc, SSD_CHUNK, g, r, p)
    da = jnp.transpose((dt * a).reshape(b, nc, SSD_CHUNK, g, r), (0, 1, 3, 4, 2))
    acs = jnp.cumsum(da, axis=-1)
    bc = bm.astype(f32).reshape(b, nc, SSD_CHUNK, g, n)
    cc = cm.astype(f32).reshape(b, nc, SSD_CHUNK, g, n)
    lower = jnp.tril(jnp.ones((SSD_CHUNK, SSD_CHUNK), dtype=bool))
    seg = acs[..., :, None] - acs[..., None, :]
    lmat = jnp.where(lower, jnp.exp(jnp.where(lower, seg, 0.0)), 0.0)
    cb = jnp.einsum('bclgn,bcsgn->bcgls', cc, bc)
    y_diag = jnp.einsum('bcgls,bcgrls,bcsgrp->bclgrp', cb, lmat, xdt)
    decay = jnp.exp(acs[..., -1:] - acs)
    states = jnp.einsum('bcsgn,bcgrs,bcsgrp->bcgrpn', bc, decay, xdt)
    dtot = jnp.exp(acs[..., -1])

    def step(s, inp):
        st, d = inp
        return s * d[..., None, None] + st, s

    final, starts = lax.scan(step, init.astype(f32).reshape(b, g, r, p, n),
                             (jnp.moveaxis(states, 1, 0), jnp.moveaxis(dtot, 1, 0)))
    starts = jnp.moveaxis(starts, 0, 1)
    y_off = jnp.einsum('bclgn,bcgrpn,bcgrl->bclgrp', cc, starts, jnp.exp(acs))
    y = (y_diag + y_off).reshape(b, l, h, p)
    return y.astype(x.dtype), final.reshape(b, h, p, n).astype(init.dtype)


def ssd_branch(z, xbc, dt_raw, lp, init_f, init_b):
    b, l, _ = z.shape
    xbc = jax.nn.silu(dwconv_centred(xbc, lp['conv_w'], lp['conv_b']))
    xs, bm, cm = jnp.split(xbc, [SSD_D_INNER, SSD_D_INNER + SSD_GROUPS * SSD_STATE], axis=-1)
    xs = xs.reshape(b, l, SSD_HEADS, SSD_HEAD_DIM)
    bm = bm.reshape(b, l, SSD_GROUPS, SSD_STATE)
    cm = cm.reshape(b, l, SSD_GROUPS, SSD_STATE)
    dt = jax.nn.softplus(dt_raw.astype(jnp.float32).reshape(b, l, 2, SSD_HEADS)
                         + lp['ssd_dt_bias'].astype(jnp.float32))
    a = -jnp.exp(lp['ssd_a_log'].astype(jnp.float32))
    y_f, s_f = ssd_scan(xs, dt[:, :, 0], a[0], bm, cm, init_f)
    y_b, s_b = ssd_scan(xs[:, ::-1], dt[:, ::-1, 1], a[1], bm[:, ::-1], cm[:, ::-1], init_b)
    y = y_f + y_b[:, ::-1] + lp['ssd_d'][:, None] * xs
    y = y.reshape(b, l, SSD_D_INNER) * jax.nn.silu(z)
    yg = y.reshape(b, l, SSD_GROUPS, SSD_D_INNER // SSD_GROUPS).astype(jnp.float32)
    yg = yg * lax.rsqrt(jnp.mean(yg * yg, axis=-1, keepdims=True) + RMS_EPS)
    y = (yg.reshape(b, l, SSD_D_INNER) * lp['ssd_norm_g'].astype(jnp.float32)).astype(z.dtype)
    return y, s_f, s_b


def ctx_attention(q, k, v):
    b, s, h, dh = q.shape
    nb = s // Q_BLOCK
    qb = jnp.moveaxis(q.reshape(b, nb, Q_BLOCK, h, dh), 1, 0)
    scale = dh ** -0.5

    def block(qi):
        sc = jnp.einsum('bqhd,bkhd->bhqk', qi, k).astype(jnp.float32) * scale
        p = jax.nn.softmax(sc, axis=-1).astype(v.dtype)
        return jnp.einsum('bhqk,bkhd->bqhd', p, v)

    o = lax.map(block, qb)
    return jnp.moveaxis(o, 0, 1).reshape(b, s, h * dh)


def na_latent(q, k, v, k_ctx, v_ctx, rpb):
    b, t, h, dh = q.shape
    rows = t // GRID_W
    kr = min(NA_WIN_ROWS, rows)
    r = jnp.arange(rows)
    row_start = jnp.clip(r - kr // 2, 0, rows - kr)
    key_rows = row_start[:, None] + jnp.arange(kr)[None, :]
    col = jnp.arange(GRID_W)
    col_start = jnp.clip(col - NA_WIN_COLS // 2, 0, GRID_W - NA_WIN_COLS)
    col_ok = (col[None, :] >= col_start[:, None]) & (col[None, :] < col_start[:, None] + NA_WIN_COLS)
    n_win = kr * GRID_W
    mask = jnp.broadcast_to(col_ok[:, None, :], (GRID_W, kr, GRID_W)).reshape(GRID_W, n_win)
    qg = q.reshape(b, rows, GRID_W, h, dh)
    kg = k.reshape(b, rows, GRID_W, h, dh)[:, key_rows].reshape(b, rows, n_win, h, dh)
    vg = v.reshape(b, rows, GRID_W, h, dh)[:, key_rows].reshape(b, rows, n_win, h, dh)
    row_off = key_rows - r[:, None] + (NA_WIN_ROWS - 1)
    col_off = jnp.clip(col[None, :] - col[:, None], -(NA_WIN_COLS - 1), NA_WIN_COLS - 1) + (NA_WIN_COLS - 1)
    bias = rpb[:, row_off[:, None, :, None], col_off[None, :, None, :]]
    bias = bias.reshape(h, rows, GRID_W, n_win).astype(jnp.float32)
    scale = dh ** -0.5
    s_win = jnp.einsum('brqhd,brkhd->bhrqk', qg, kg).astype(jnp.float32) * scale + bias[None]
    s_win = jnp.where(mask, s_win, -jnp.inf)
    s_ctx = jnp.einsum('brqhd,bkhd->bhrqk', qg, k_ctx).astype(jnp.float32) * scale
    p = jax.nn.softmax(jnp.concatenate([s_win, s_ctx], axis=-1), axis=-1).astype(v.dtype)
    o = (jnp.einsum('bhrqk,brkhd->brqhd', p[..., :n_win], vg)
         + jnp.einsum('bhrqk,bkhd->brqhd', p[..., n_win:], v_ctx))
    return o.reshape(b, t, h * dh)


def peer(h, w_q, sub_keys, u, v):
    b, l, d = h.shape
    kk = PEER_TOPK
    q = (h @ w_q).reshape(b, l, PEER_HEADS, 2, PEER_KEY_DIM // 2)
    s = jnp.einsum('blhpk,hpnk->blhpn', q, sub_keys).astype(jnp.float32)
    sv, si = lax.top_k(s, kk)
    cand = sv[..., 0, :, None] + sv[..., 1, None, :]
    cv, ci = lax.top_k(cand.reshape(b, l, PEER_HEADS, kk * kk), kk)
    i1 = jnp.take_along_axis(si[..., 0, :], ci // kk, axis=-1)
    i2 = jnp.take_along_axis(si[..., 1, :], ci % kk, axis=-1)
    expert = i1 * PEER_N_KEYS + i2
    gate = jax.nn.softmax(cv, axis=-1).astype(h.dtype)
    t = b * l
    nb = t // PEER_TOKEN_BLOCK
    hb = h.reshape(nb, PEER_TOKEN_BLOCK, d)
    eb = expert.reshape(nb, PEER_TOKEN_BLOCK, PEER_HEADS * kk)
    gb = gate.reshape(nb, PEER_TOKEN_BLOCK, PEER_HEADS * kk)

    def block(args):
        hx, ex, gx = args
        act = jax.nn.gelu(jnp.einsum('tkd,td->tk', u[ex], hx), approximate=False)
        return jnp.einsum('tk,tkd->td', gx * act, v[ex])

    out = lax.map(block, (hb, eb, gb))
    return out.reshape(b, l, d)


def trunk_layer(x, lp, cvec, ctx_cache=None):
    b, l, _ = x.shape
    m = jnp.dot(jax.nn.silu(cvec), lp['ada_w']) + lp['ada_b']
    sh1, sc1, g1, sh2, sc2, g2 = [t[:, None, :] for t in jnp.split(m, 6, axis=-1)]
    h = rmsnorm(x, lp['norm1_g']) * (1 + sc1) + sh1
    q, k, v, z, xbc, dt_raw, gates = jnp.split(h @ lp['w_in'], IN_SPLITS, axis=-1)
    q = q.reshape(b, l, NA_HEADS, NA_HEAD_DIM)
    k = k.reshape(b, l, NA_HEADS, NA_HEAD_DIM)
    v = v.reshape(b, l, NA_HEADS, NA_HEAD_DIM)
    if ctx_cache is None:
        na = ctx_attention(q, k, v)
        init_f = jnp.zeros((b, SSD_HEADS, SSD_HEAD_DIM, SSD_STATE), x.dtype)
        init_b = init_f
    else:
        k_ctx, v_ctx, init_f, init_b = ctx_cache
        na = na_latent(q, k, v, k_ctx, v_ctx, lp['na_rpb'])
    ssd, s_f, s_b = ssd_branch(z, xbc, dt_raw, lp, init_f, init_b)
    g_na, g_ssd = jnp.split(jax.nn.sigmoid(gates), 2, axis=-1)
    mixed = (g_na * (na @ lp['w_na_proj']) + g_ssd * (ssd @ lp['w_ssd_proj'])) @ lp['w_out']
    x = x + g1 * mixed
    h2 = rmsnorm(x, lp['norm2_g']) * (1 + sc2) + sh2
    x = x + g2 * peer(h2, lp['peer_wq'], lp['peer_keys'], lp['peer_u'], lp['peer_v'])
    return x, (k, v, s_f, s_b)


def setup_inputs(seed: int = 0) -> dict:
    key = jax.random.key(seed)
    ks = jax.random.split(key, 32)
    nrm = lambda i, shape, s=1.0: jax.random.normal(ks[i], shape, jnp.float32) * s
    dt0 = jnp.exp(jax.random.uniform(ks[15], (DEPTH, 2, SSD_HEADS), jnp.float32,
                                     np.log(1e-3), np.log(1e-1)))
    return {
        'x_prompt': nrm(0, (BATCH, SEQ, D_MODEL)),
        'x_sample': nrm(1, (DEC_BATCH, DEC_SEQ, D_MODEL)),
        'c': nrm(2, (DEC_BATCH, D_MODEL)),
        'c_ctx': nrm(3, (D_MODEL,)),
        'cache_na_k': nrm(4, (DEC_BATCH, DEPTH, PAST_LEN, NA_HEADS, NA_HEAD_DIM)),
        'cache_na_v': nrm(5, (DEC_BATCH, DEPTH, PAST_LEN, NA_HEADS, NA_HEAD_DIM)),
        'state_ssd_fwd': nrm(6, (DEC_BATCH, DEPTH, SSD_HEADS, SSD_HEAD_DIM, SSD_STATE), 0.5),
        'state_ssd_bwd': nrm(7, (DEC_BATCH, DEPTH, SSD_HEADS, SSD_HEAD_DIM, SSD_STATE), 0.5),
        'ada_w': nrm(8, (DEPTH, D_MODEL, 6 * D_MODEL), 0.5 * D_MODEL ** -0.5),
        'ada_b': nrm(9, (DEPTH, 6 * D_MODEL), 0.02),
        'norm1_g': 1.0 + nrm(10, (DEPTH, D_MODEL), 0.02),
        'norm2_g': 1.0 + nrm(11, (DEPTH, D_MODEL), 0.02),
        'w_in': nrm(12, (DEPTH, D_MODEL, IN_COLS), D_MODEL ** -0.5),
        'conv_w': nrm(13, (DEPTH, SSD_CONV, SSD_CONV_CH), SSD_CONV ** -0.5),
        'conv_b': nrm(14, (DEPTH, SSD_CONV_CH), 0.02),
        'na_rpb': nrm(16, (DEPTH, NA_HEADS, 2 * NA_WIN_ROWS - 1, 2 * NA_WIN_COLS - 1), 0.5),
        'ssd_a_log': jnp.log(jax.random.uniform(ks[17], (DEPTH, 2, SSD_HEADS), jnp.float32, 1.0, 16.0)),
        'ssd_dt_bias': dt0 + jnp.log(-jnp.expm1(-dt0)),
        'ssd_d': 1.0 + nrm(18, (DEPTH, SSD_HEADS), 0.1),
        'ssd_norm_g': 1.0 + nrm(19, (DEPTH, SSD_D_INNER), 0.02),
        'w_na_proj': nrm(20, (DEPTH, NA_WIDTH, D_MODEL), NA_WIDTH ** -0.5),
        'w_ssd_proj': nrm(21, (DEPTH, SSD_D_INNER, D_MODEL), SSD_D_INNER ** -0.5),
        'w_out': nrm(22, (DEPTH, D_MODEL, D_MODEL), D_MODEL ** -0.5),
        'peer_wq': nrm(23, (DEPTH, D_MODEL, PEER_HEADS * PEER_KEY_DIM), D_MODEL ** -0.5),
        'peer_keys': nrm(24, (DEPTH, PEER_HEADS, 2, PEER_N_KEYS, PEER_KEY_DIM // 2), (PEER_KEY_DIM // 2) ** -0.5),
        'peer_u': nrm(25, (DEPTH, PEER_N_EXPERTS, D_MODEL), D_MODEL ** -0.5),
        'peer_v': nrm(26, (DEPTH, PEER_N_EXPERTS, D_MODEL)),
        'final_g': 1.0 + nrm(27, (D_MODEL,), 0.02),
    }


def reference(x_prompt, x_sample, c, c_ctx, cache_na_k, cache_na_v, state_ssd_fwd, state_ssd_bwd,
              ada_w, ada_b, norm1_g, norm2_g, w_in, conv_w, conv_b, na_rpb, ssd_a_log, ssd_dt_bias,
              ssd_d, ssd_norm_g, w_na_proj, w_ssd_proj, w_out, peer_wq, peer_keys, peer_u, peer_v,
              final_g):
    yp = x_prompt
    ys = x_sample
    new_k, new_v, new_f, new_b = [], [], [], []
    for i in range(DEPTH):
        lp = {'ada_w': ada_w[i], 'ada_b': ada_b[i], 'norm1_g': norm1_g[i], 'norm2_g': norm2_g[i],
              'w_in': w_in[i], 'conv_w': conv_w[i], 'conv_b': conv_b[i], 'na_rpb': na_rpb[i],
              'ssd_a_log': ssd_a_log[i], 'ssd_dt_bias': ssd_dt_bias[i], 'ssd_d': ssd_d[i],
              'ssd_norm_g': ssd_norm_g[i], 'w_na_proj': w_na_proj[i], 'w_ssd_proj': w_ssd_proj[i],
              'w_out': w_out[i], 'peer_wq': peer_wq[i], 'peer_keys': peer_keys[i],
              'peer_u': peer_u[i], 'peer_v': peer_v[i]}
        yp, (k_i, v_i, sf_i, sb_i) = trunk_layer(yp, lp, c_ctx[None, :])
        new_k.append(k_i)
        new_v.append(v_i)
        new_f.append(sf_i)
        new_b.append(sb_i)
        ys, _ = trunk_layer(ys, lp, c, (cache_na_k[:, i], cache_na_v[:, i],
                                        state_ssd_fwd[:, i], state_ssd_bwd[:, i]))
    y_prompt = rmsnorm(yp, final_g)
    y_sample = rmsnorm(ys, final_g)
    return (y_prompt, y_sample, jnp.stack(new_k, axis=1), jnp.stack(new_v, axis=1),
            jnp.stack(new_f, axis=1), jnp.stack(new_b, axis=1))
```

```python
import functools

import jax
import jax.numpy as jnp
from jax import lax
from jax.experimental import pallas as pl
from jax.experimental.pallas import tpu as pltpu

F32 = jnp.float32
BF16 = jnp.bfloat16

D_MODEL = 2048
SEQ = 256
DEC_SEQ = 1024
GRID_W = 64
NA_HEADS = 16
NA_HEAD_DIM = 128
NA_WIN_ROWS = 8
NA_WIN_COLS = 16
SSD_D_INNER = 4096
SSD_HEAD_DIM = 64
SSD_HEADS = 64
SSD_GROUPS = 8
SSD_STATE = 128
SSD_CONV = 5
SSD_CHUNK = 128
SSD_CONV_CH = SSD_D_INNER + 2 * SSD_GROUPS * SSD_STATE
HEADS_PER_GROUP = SSD_HEADS // SSD_GROUPS
GROUP_CH = SSD_D_INNER // SSD_GROUPS
PEER_HEADS = 8
PEER_HALF = 128
PEER_N_KEYS = 128
PEER_TOPK = 16
RMS_EPS = 1e-6
NEG_BIG = -1e30
MOD_ROWS = 8
VMEM_LIMIT_MB = 48

NT_DIMS = (((1,), (1,)), ((), ()))


def _cp(sem, vmem_mb=VMEM_LIMIT_MB):
    return pltpu.CompilerParams(dimension_semantics=sem, vmem_limit_bytes=vmem_mb * 1024 * 1024)


def _mod_row(i, tm, n_ctx_rows):
    n_ctx_tiles = n_ctx_rows // tm
    return jnp.where(i < n_ctx_tiles, 0, 1 + (i - n_ctx_tiles) // (DEC_SEQ // tm))


def _silu(x):
    return x * jax.nn.sigmoid(x)


def _softplus(x):
    return jnp.maximum(x, 0.0) + jnp.log1p(jnp.exp(-jnp.abs(x)))


def _split3(x):
    hi = x.astype(BF16)
    r1 = x - hi.astype(F32)
    mid = r1.astype(BF16)
    lo = (r1 - mid.astype(F32)).astype(BF16)
    return hi, mid, lo


def _dot_f32_by_01(x, m01):
    return sum(jnp.dot(p, m01, preferred_element_type=F32) for p in _split3(x))


def _dot_01_by_f32(m01, x):
    return sum(jnp.dot(m01, p, preferred_element_type=F32) for p in _split3(x))


def _ada_kernel(c_ref, w_ref, b_ref, o_ref):
    s = _silu(c_ref[...])
    o_ref[...] = jnp.dot(s, w_ref[...], preferred_element_type=F32,
                         precision=lax.Precision.HIGHEST) + b_ref[...]


def _ada(cvec, ada_w, ada_b):
    n = ada_w.shape[1]
    tn = 1024
    return pl.pallas_call(
        _ada_kernel,
        grid=(n // tn,),
        in_specs=[pl.BlockSpec((MOD_ROWS, D_MODEL), lambda j: (0, 0)),
                  pl.BlockSpec((D_MODEL, tn), lambda j: (0, j)),
                  pl.BlockSpec((1, tn), lambda j: (0, j))],
        out_specs=pl.BlockSpec((MOD_ROWS, tn), lambda j: (0, j)),
        out_shape=jax.ShapeDtypeStruct((MOD_ROWS, n), F32),
        compiler_params=_cp(("parallel",)),
        name="ada_mod",
    )(cvec, ada_w, ada_b)


def _norm_mod_kernel(x_ref, g_ref, sc_ref, sh_ref, o_ref):
    x = x_ref[...]
    y = x * lax.rsqrt(jnp.mean(x * x, axis=-1, keepdims=True) + RMS_EPS) * g_ref[...]
    o_ref[...] = (y * (1.0 + sc_ref[...]) + sh_ref[...]).astype(o_ref.dtype)


def _norm_mod(x, g, mod3, sc_chunk, sh_chunk, n_ctx_rows):
    t = x.shape[0]
    tm = 256
    row = functools.partial(_mod_row, tm=tm, n_ctx_rows=n_ctx_rows)
    return pl.pallas_call(
        _norm_mod_kernel,
        grid=(t // tm,),
        in_specs=[pl.BlockSpec((tm, D_MODEL), lambda i: (i, 0)),
                  pl.BlockSpec((1, D_MODEL), lambda i: (0, 0)),
                  pl.BlockSpec((None, 1, D_MODEL), lambda i: (row(i), 0, sc_chunk)),
                  pl.BlockSpec((None, 1, D_MODEL), lambda i: (row(i), 0, sh_chunk))],
        out_specs=pl.BlockSpec((tm, D_MODEL), lambda i: (i, 0)),
        out_shape=jax.ShapeDtypeStruct((t, D_MODEL), BF16),
        compiler_params=_cp(("parallel",)),
        name="norm_mod",
    )(x, g, mod3, mod3)


def _mm_kernel(a_ref, b_ref, o_ref):
    o_ref[...] = jnp.dot(a_ref[...], b_ref[...], preferred_element_type=F32).astype(o_ref.dtype)


def _mm(a, b, out_dtype, tm=1024, tn=1024, name="mm"):
    m, k = a.shape
    n = b.shape[1]
    tm = min(tm, m)
    tn = min(tn, n)
    return pl.pallas_call(
        _mm_kernel,
        grid=(m // tm, n // tn),
        in_specs=[pl.BlockSpec((tm, k), lambda i, j: (i, 0)),
                  pl.BlockSpec((k, tn), lambda i, j: (0, j))],
        out_specs=pl.BlockSpec((tm, tn), lambda i, j: (i, j)),
        out_shape=jax.ShapeDtypeStruct((m, n), out_dtype),
        compiler_params=_cp(("parallel", "parallel")),
        name=name,
    )(a, b)


def _ctx_attn_kernel(q_ref, k_ref, v_ref, o_ref):
    scale = NA_HEAD_DIM ** -0.5
    for h in range(NA_HEADS):
        sl = slice(h * NA_HEAD_DIM, (h + 1) * NA_HEAD_DIM)
        q = q_ref[:, sl]
        k = k_ref[:, sl].astype(BF16)
        v = v_ref[:, sl].astype(BF16)
        s = lax.dot_general(q, k, NT_DIMS, preferred_element_type=F32) * scale
        p = jnp.exp(s - jnp.max(s, axis=-1, keepdims=True))
        l = jnp.sum(p, axis=-1, keepdims=True)
        o = jnp.dot(p.astype(BF16), v, preferred_element_type=F32) / l
        o_ref[:, sl] = o.astype(o_ref.dtype)


def _ctx_attention(q, k, v, n_ctx_batch):
    width = NA_HEADS * NA_HEAD_DIM
    spec = pl.BlockSpec((SEQ, width), lambda b: (b, 0))
    return pl.pallas_call(
        _ctx_attn_kernel,
        grid=(n_ctx_batch,),
        in_specs=[spec, spec, spec],
        out_specs=spec,
        out_shape=jax.ShapeDtypeStruct((n_ctx_batch * SEQ, width), BF16),
        compiler_params=_cp(("parallel",)),
        name="ctx_attn",
    )(q, k, v)


def _na_lat_kernel(q_ref, k_ref, v_ref, kc_ref, vc_ref, bias_ref, o_ref):
    scale = NA_HEAD_DIM ** -0.5
    rows = DEC_SEQ // GRID_W
    kc = kc_ref[...].astype(BF16)
    vc = vc_ref[...].astype(BF16)
    for r in range(rows):
        rs = min(max(r - NA_WIN_ROWS // 2, 0), rows - NA_WIN_ROWS)
        q = q_ref[r * GRID_W:(r + 1) * GRID_W, :]
        win = slice(rs * GRID_W, (rs + NA_WIN_ROWS) * GRID_W)
        kw = k_ref[win, :].astype(BF16)
        vw = v_ref[win, :].astype(BF16)
        sw = lax.dot_general(q, kw, NT_DIMS, preferred_element_type=F32) * scale + bias_ref[r]
        sc = lax.dot_general(q, kc, NT_DIMS, preferred_element_type=F32) * scale
        m = jnp.maximum(jnp.max(sw, axis=-1, keepdims=True), jnp.max(sc, axis=-1, keepdims=True))
        pw = jnp.exp(sw - m)
        pc = jnp.exp(sc - m)
        l = jnp.sum(pw, axis=-1, keepdims=True) + jnp.sum(pc, axis=-1, keepdims=True)
        o = (jnp.dot(pw.astype(BF16), vw, preferred_element_type=F32)
             + jnp.dot(pc.astype(BF16), vc, preferred_element_type=F32)) / l
        o_ref[r * GRID_W:(r + 1) * GRID_W, :] = o.astype(o_ref.dtype)


def _na_bias_table(rpb):
    rows = DEC_SEQ // GRID_W
    kr = NA_WIN_ROWS
    r = jnp.arange(rows)
    row_start = jnp.clip(r - kr // 2, 0, rows - kr)
    key_rows = row_start[:, None] + jnp.arange(kr)[None, :]
    col = jnp.arange(GRID_W)
    col_start = jnp.clip(col - NA_WIN_COLS // 2, 0, GRID_W - NA_WIN_COLS)
    col_ok = (col[None, :] >= col_start[:, None]) & (col[None, :] < col_start[:, None] + NA_WIN_COLS)
    row_off = key_rows - r[:, None] + (NA_WIN_ROWS - 1)
    col_off = jnp.clip(col[None, :] - col[:, None], -(NA_WIN_COLS - 1), NA_WIN_COLS - 1) + (NA_WIN_COLS - 1)
    bias = rpb[:, row_off[:, None, :, None], col_off[None, :, None, :]]
    mask = col_ok[None, None, :, None, :]
    bias = jnp.where(mask, bias.astype(F32), NEG_BIG)
    return bias.reshape(rpb.shape[0], rows, GRID_W, kr * GRID_W)


def _na_latent(q, k, v, k_ctx, v_ctx, bias, n_ctx_rows, n_lat_batch):
    off = n_ctx_rows // DEC_SEQ
    tok = pl.BlockSpec((DEC_SEQ, NA_HEAD_DIM), lambda b, h: (off + b, h))
    ctx = pl.BlockSpec((None, k_ctx.shape[1], NA_HEAD_DIM), lambda b, h: (b, 0, h))
    return pl.pallas_call(
        _na_lat_kernel,
        grid=(n_lat_batch, NA_HEADS),
        in_specs=[tok, tok, tok, ctx, ctx,
                  pl.BlockSpec((None,) + bias.shape[1:], lambda b, h: (h, 0, 0, 0))],
        out_specs=pl.BlockSpec((DEC_SEQ, NA_HEAD_DIM), lambda b, h: (b, h)),
        out_shape=jax.ShapeDtypeStruct((n_lat_batch * DEC_SEQ, NA_HEADS * NA_HEAD_DIM), BF16),
        compiler_params=_cp(("parallel", "parallel")),
        name="na_latent",
    )(q, k, v, k_ctx, v_ctx, bias)


def _ssd_kernel(*refs, seq, has_init):
    n_in = 20 if has_init else 18
    (xs_ref, b_ref, c_ref, z_ref, dt_ref, dtt_ref, cwx_ref, cwb_ref, cwc_ref,
     cbx_ref, cbb_ref, cbc_ref, dtb_ref, dtbt_ref, al_ref, alt_ref, dexp_ref, ng_ref) = refs[:18]
    if has_init:
        if_ref, ib_ref = refs[18:20]
        (y_ref,) = refs[n_in:n_in + 1]
        scratch = refs[n_in + 1:]
    else:
        y_ref, sfo_ref, sbo_ref = refs[n_in:n_in + 3]
        scratch = refs[n_in + 3:]
    pad_s, xs_s, bb_s, cc_s, dt_s, cumc_s, cumr_s, yacc_s, sf_s, sb_s = scratch

    nc = seq // SSD_CHUNK
    ck = SSD_CHUNK
    halo = 8

    def conv_silu(dst_ref, src_ref, w_ref, bias_ref, width):
        pad_s[0:halo, 0:width] = jnp.zeros((halo, width), F32)
        pad_s[seq + halo:seq + 2 * halo, 0:width] = jnp.zeros((halo, width), F32)
        pad_s[halo:seq + halo, 0:width] = src_ref[...]
        for ci in range(nc):
            base = halo - SSD_CONV // 2 + ci * ck
            acc = bias_ref[...] + pad_s[base:base + ck, 0:width] * w_ref[0:1, :]
            for tap in range(1, SSD_CONV):
                acc = acc + pad_s[base + tap:base + tap + ck, 0:width] * w_ref[tap:tap + 1, :]
            dst_ref[ci * ck:(ci + 1) * ck, :] = _silu(acc)

    conv_silu(xs_s, xs_ref, cwx_ref, cbx_ref, GROUP_CH)
    conv_silu(bb_s, b_ref, cwb_ref, cbb_ref, SSD_STATE)
    conv_silu(cc_s, c_ref, cwc_ref, cbc_ref, SSD_STATE)

    li = lax.broadcasted_iota(jnp.int32, (ck, ck), 0)
    si = lax.broadcasted_iota(jnp.int32, (ck, ck), 1)
    lower = li >= si
    upper = si >= li
    tri_lo = jnp.where(lower, 1.0, 0.0).astype(BF16)
    tri_up = jnp.where(upper, 1.0, 0.0).astype(BF16)
    nh2 = 2 * HEADS_PER_GROUP

    dt_s[...] = _softplus(dt_ref[...] + dtb_ref[...])
    a_row = -jnp.exp(al_ref[...])
    dtt = _softplus(dtt_ref[...] + dtbt_ref[...])
    dat = dtt * (-jnp.exp(alt_ref[...]))
    col_is_fwd = lax.broadcasted_iota(jnp.int32, (ck, nh2), 1) < HEADS_PER_GROUP
    row_is_fwd = lax.broadcasted_iota(jnp.int32, (nh2, ck), 0) < HEADS_PER_GROUP
    for c in range(nc):
        da = dt_s[c * ck:(c + 1) * ck, :] * a_row
        cumc_s[c] = jnp.where(col_is_fwd, _dot_01_by_f32(tri_lo, da), _dot_01_by_f32(tri_up, da))
        dat_c = dat[:, c * ck:(c + 1) * ck]
        cumr_s[c] = jnp.where(row_is_fwd, _dot_f32_by_01(dat_c, tri_up), _dot_f32_by_01(dat_c, tri_lo))

    erow = lax.broadcasted_iota(jnp.int32, (nh2, GROUP_CH), 0)
    ecol = lax.broadcasted_iota(jnp.int32, (nh2, GROUP_CH), 1) // SSD_HEAD_DIM
    expand = (jnp.where(erow == ecol, 1.0, 0.0).astype(BF16),
              jnp.where(erow == ecol + HEADS_PER_GROUP, 1.0, 0.0).astype(BF16))
    lane_lo = lax.broadcasted_iota(jnp.int32, (ck, 2 * SSD_HEAD_DIM), 1) < SSD_HEAD_DIM

    yacc_s[...] = jnp.zeros_like(yacc_s)
    if has_init:
        sf_s[...] = if_ref[...].reshape(GROUP_CH, SSD_STATE).T
        sb_s[...] = ib_ref[...].reshape(GROUP_CH, SSD_STATE).T
    else:
        sf_s[...] = jnp.zeros_like(sf_s)
        sb_s[...] = jnp.zeros_like(sb_s)

    def chunk_step(i, carry):
        for d in (0, 1):
            c = i if d == 0 else nc - 1 - i
            r0 = pl.multiple_of(c * ck, ck)
            xs_c = xs_s[pl.ds(r0, ck), :]
            b_c = bb_s[pl.ds(r0, ck), :]
            c_c = cc_s[pl.ds(r0, ck), :].astype(BF16)
            cumc = jnp.where(col_is_fwd if d == 0 else ~col_is_fwd, cumc_s[c], 0.0)
            cumr = cumr_s[c]
            e01 = expand[d]
            state_ref = sf_s if d == 0 else sb_s
            tri_mask = lower if d == 0 else upper

            cb = lax.dot_general(c_c, b_c.astype(BF16), NT_DIMS, preferred_element_type=F32)
            xdt = xs_c * _dot_f32_by_01(dt_s[pl.ds(r0, ck), :], e01)
            xdt_bf = xdt.astype(BF16)
            total = cumc[ck - 1:ck, :] if d == 0 else cumc[0:1, :]
            eacs_x = _dot_f32_by_01(jnp.exp(cumc), e01)
            decay_x = _dot_f32_by_01(jnp.exp(total - cumc), e01)
            dtot_x = _dot_f32_by_01(jnp.broadcast_to(jnp.exp(total), (8, nh2)), e01)[0:1, :]

            state = state_ref[...]
            y = jnp.dot(c_c, state.astype(BF16), preferred_element_type=F32) * eacs_x
            state_ref[...] = state * dtot_x + jnp.dot(
                b_c.T.astype(BF16), (decay_x * xdt).astype(BF16), preferred_element_type=F32)

            pieces = []
            for hp in range(HEADS_PER_GROUP // 2):
                x_pair = xdt_bf[:, hp * 2 * SSD_HEAD_DIM:(hp + 1) * 2 * SSD_HEAD_DIM]
                outs = []
                for sub in range(2):
                    j = d * HEADS_PER_GROUP + hp * 2 + sub
                    seg = cumc[:, j:j + 1] - cumr[j:j + 1, :]
                    lmat = jnp.where(tri_mask, jnp.exp(jnp.where(tri_mask, seg, 0.0)), 0.0)
                    outs.append(jnp.dot((cb * lmat).astype(BF16), x_pair, preferred_element_type=F32))
                pieces.append(jnp.where(lane_lo, outs[0], outs[1]))
            y = y + jnp.concatenate(pieces, axis=1)
            yacc_s[pl.ds(r0, ck), :] = yacc_s[pl.ds(r0, ck), :] + y
        return carry

    lax.fori_loop(0, nc, chunk_step, 0)

    for ci in range(nc):
        rows = slice(ci * ck, (ci + 1) * ck)
        zz = z_ref[rows, :]
        y = (yacc_s[rows, :] + dexp_ref[...] * xs_s[rows, :]) * _silu(zz)
        y = y * lax.rsqrt(jnp.mean(y * y, axis=-1, keepdims=True) + RMS_EPS) * ng_ref[...]
        y_ref[rows, :] = y.astype(y_ref.dtype)

    if not has_init:
        sfo_ref[...] = sf_s[...].T.reshape(HEADS_PER_GROUP, SSD_HEAD_DIM, SSD_STATE)
        sbo_ref[...] = sb_s[...].T.reshape(HEADS_PER_GROUP, SSD_HEAD_DIM, SSD_STATE)


def _ssd(xbc, z, dtg, dtgt, conv_w8, conv_b, dtb, dtbt, alog, alogt, dexp, norm_g,
         row_start, nb, seq, init=None):
    off = row_start // seq
    g_b = SSD_D_INNER // SSD_STATE
    g_c = g_b + SSD_GROUPS
    has_init = init is not None
    in_specs = [
        pl.BlockSpec((seq, GROUP_CH), lambda b, g: (off + b, g)),
        pl.BlockSpec((seq, SSD_STATE), lambda b, g: (off + b, g_b + g)),
        pl.BlockSpec((seq, SSD_STATE), lambda b, g: (off + b, g_c + g)),
        pl.BlockSpec((seq, GROUP_CH), lambda b, g: (off + b, g)),
        pl.BlockSpec((None, seq, 2 * HEADS_PER_GROUP), lambda b, g: (g, off + b, 0)),
        pl.BlockSpec((None, 2 * HEADS_PER_GROUP, seq), lambda b, g: (g, 0, off + b)),
        pl.BlockSpec((8, GROUP_CH), lambda b, g: (0, g)),
        pl.BlockSpec((8, SSD_STATE), lambda b, g: (0, g_b + g)),
        pl.BlockSpec((8, SSD_STATE), lambda b, g: (0, g_c + g)),
        pl.BlockSpec((1, GROUP_CH), lambda b, g: (0, g)),
        pl.BlockSpec((1, SSD_STATE), lambda b, g: (0, g_b + g)),
        pl.BlockSpec((1, SSD_STATE), lambda b, g: (0, g_c + g)),
        pl.BlockSpec((None, 1, 2 * HEADS_PER_GROUP), lambda b, g: (g, 0, 0)),
        pl.BlockSpec((None, 2 * HEADS_PER_GROUP, 1), lambda b, g: (g, 0, 0)),
        pl.BlockSpec((None, 1, 2 * HEADS_PER_GROUP), lambda b, g: (g, 0, 0)),
        pl.BlockSpec((None, 2 * HEADS_PER_GROUP, 1), lambda b, g: (g, 0, 0)),
        pl.BlockSpec((1, GROUP_CH), lambda b, g: (0, g)),
        pl.BlockSpec((1, GROUP_CH), lambda b, g: (0, g)),
    ]
    args = [xbc, xbc, xbc, z, dtg, dtgt, conv_w8, conv_w8, conv_w8, conv_b, conv_b, conv_b,
            dtb, dtbt, alog, alogt, dexp, norm_g]
    state_spec = pl.BlockSpec((None, HEADS_PER_GROUP, SSD_HEAD_DIM, SSD_STATE), lambda b, g: (b, g, 0, 0))
    y_spec = pl.BlockSpec((seq, GROUP_CH), lambda b, g: (b, g))
    y_shape = jax.ShapeDtypeStruct((nb * seq, SSD_D_INNER), BF16)
    if has_init:
        in_specs += [state_spec, state_spec]
        args += list(init)
        out_specs = y_spec
        out_shape = y_shape
    else:
        st_shape = jax.ShapeDtypeStruct((nb, SSD_HEADS, SSD_HEAD_DIM, SSD_STATE), F32)
        out_specs = [y_spec, state_spec, state_spec]
        out_shape = [y_shape, st_shape, st_shape]
    nc = seq // SSD_CHUNK
    scratch = [
        pltpu.VMEM((seq + 16, GROUP_CH), F32),
        pltpu.VMEM((seq, GROUP_CH), F32),
        pltpu.VMEM((seq, SSD_STATE), F32),
        pltpu.VMEM((seq, SSD_STATE), F32),
        pltpu.VMEM((seq, 2 * HEADS_PER_GROUP), F32),
        pltpu.VMEM((nc, SSD_CHUNK, 2 * HEADS_PER_GROUP), F32),
        pltpu.VMEM((nc, 2 * HEADS_PER_GROUP, SSD_CHUNK), F32),
        pltpu.VMEM((seq, GROUP_CH), F32),
        pltpu.VMEM((SSD_STATE, GROUP_CH), F32),
        pltpu.VMEM((SSD_STATE, GROUP_CH), F32),
    ]
    return pl.pallas_call(
        functools.partial(_ssd_kernel, seq=seq, has_init=has_init),
        grid=(nb, SSD_GROUPS),
        in_specs=in_specs,
        out_specs=out_specs,
        out_shape=out_shape,
        scratch_shapes=scratch,
        compiler_params=_cp(("parallel", "parallel")),
        name="ssd_lat" if has_init else "ssd_ctx",
    )(*args)


def _mix_kernel(na_ref, ssd_ref, wna_ref, wssd_ref, gna_ref, gssd_ref, o_ref):
    a = jnp.dot(na_ref[...], wna_ref[...], preferred_element_type=F32)
    s = jnp.dot(ssd_ref[...], wssd_ref[...], preferred_element_type=F32)
    o = jax.nn.sigmoid(gna_ref[...]) * a + jax.nn.sigmoid(gssd_ref[...]) * s
    o_ref[...] = o.astype(o_ref.dtype)


def _mix(na, ssd, w_na, w_ssd, gates):
    t = na.shape[0]
    tm, tn = 512, 512
    nj = D_MODEL // tn
    return pl.pallas_call(
        _mix_kernel,
        grid=(t // tm, nj),
        in_specs=[pl.BlockSpec((tm, na.shape[1]), lambda i, j: (i, 0)),
                  pl.BlockSpec((tm, ssd.shape[1]), lambda i, j: (i, 0)),
                  pl.BlockSpec((w_na.shape[0], tn), lambda i, j: (0, j)),
                  pl.BlockSpec((w_ssd.shape[0], tn), lambda i, j: (0, j)),
                  pl.BlockSpec((tm, tn), lambda i, j: (i, j)),
                  pl.BlockSpec((tm, tn), lambda i, j: (i, nj + j))],
        out_specs=pl.BlockSpec((tm, tn), lambda i, j: (i, j)),
        out_shape=jax.ShapeDtypeStruct((t, D_MODEL), BF16),
        compiler_params=_cp(("parallel", "parallel")),
        name="branch_mix",
    )(na, ssd, w_na, w_ssd, gates, gates)


def _outproj_kernel(a_ref, w_ref, x_ref, g_ref, o_ref):
    y = jnp.dot(a_ref[...], w_ref[...], preferred_element_type=F32)
    o_ref[...] = x_ref[...] + g_ref[...] * y


def _outproj_residual(a, w, x, mod3, gate_chunk, n_ctx_rows):
    t = a.shape[0]
    tm, tn = 1024, 1024
    nj = D_MODEL // tn
    row = functools.partial(_mod_row, tm=tm, n_ctx_rows=n_ctx_rows)
    return pl.pallas_call(
        _outproj_kernel,
        grid=(t // tm, nj),
        in_specs=[pl.BlockSpec((tm, a.shape[1]), lambda i, j: (i, 0)),
                  pl.BlockSpec((w.shape[0], tn), lambda i, j: (0, j)),
                  pl.BlockSpec((tm, tn), lambda i, j: (i, j)),
                  pl.BlockSpec((None, 1, tn), lambda i, j: (row(i), 0, gate_chunk * nj + j))],
        out_specs=pl.BlockSpec((tm, tn), lambda i, j: (i, j)),
        out_shape=jax.ShapeDtypeStruct((t, D_MODEL), F32),
        compiler_params=_cp(("parallel", "parallel")),
        name="out_proj",
    )(a, w, x, mod3)


def _peer_route_kernel(h2t_ref, wqt_ref, keys_ref, s_ref, e_ref, tau_ref, qt_s, slab_s, sv_s, cand_s):
    tt = h2t_ref.shape[1]
    n_sub = 2 * PEER_HEADS
    kk = PEER_TOPK
    qt_s[...] = jnp.dot(wqt_ref[...], h2t_ref[...], preferred_element_type=F32).astype(BF16)

    def half_step(c, carry):
        r0 = pl.multiple_of(c * PEER_HALF, PEER_HALF)
        sc = jnp.dot(keys_ref[c], qt_s[pl.ds(r0, PEER_HALF), :], preferred_element_type=F32)
        s_ref[c] = sc
        slab_s[...] = sc

        def extract(r, inner):
            cur = slab_s[...]
            m = jnp.max(cur, axis=0, keepdims=True)
            sv_s[c, pl.ds(r, 1), :] = m
            slab_s[...] = jnp.where(cur == m, -jnp.inf, cur)
            return inner

        lax.fori_loop(0, kk, extract, 0)
        return carry

    lax.fori_loop(0, n_sub, half_step, 0)

    def head_step(h, carry):
        sv1 = sv_s[2 * h]
        sv2 = sv_s[2 * h + 1]
        for a in range(kk):
            cand_s[a * kk:(a + 1) * kk, :] = sv1[a:a + 1, :] + sv2
        m0 = sv1[0:1, :] + sv2[0:1, :]

        def extract(r, st):
            taken, tau, zsum = st
            cur = cand_s[...]
            m = jnp.max(cur, axis=0, keepdims=True)
            eq = cur == m
            cnt = jnp.sum(jnp.where(eq, 1.0, 0.0), axis=0, keepdims=True)
            active = taken < kk
            tau = jnp.where(active, m, tau)
            zsum = zsum + jnp.where(active, cnt * jnp.exp(m - m0), 0.0)
            taken = taken + jnp.where(active, cnt, 0.0)
            cand_s[...] = jnp.where(eq, -jnp.inf, cur)
            return taken, tau, zsum

        zero = jnp.zeros((1, tt), F32)
        _, tau, zsum = lax.fori_loop(0, kk, extract, (zero, m0, zero))
        tau_ref[pl.ds(h, 1), :] = tau
        e_ref[2 * h] = jnp.exp(s_ref[2 * h] - sv1[0:1, :]) / zsum
        e_ref[2 * h + 1] = jnp.exp(s_ref[2 * h + 1] - sv2[0:1, :])
        return carry

    lax.fori_loop(0, PEER_HEADS, head_step, 0)


def _peer_route(h2t, wqt, keys):
    t = h2t.shape[1]
    tt = 512
    n_sub = 2 * PEER_HEADS
    big = pl.BlockSpec((n_sub, PEER_N_KEYS, tt), lambda i: (0, 0, i))
    return pl.pallas_call(
        _peer_route_kernel,
        grid=(t // tt,),
        in_specs=[pl.BlockSpec((D_MODEL, tt), lambda i: (0, i)),
                  pl.BlockSpec(wqt.shape, lambda i: (0, 0)),
                  pl.BlockSpec(keys.shape, lambda i: (0, 0, 0))],
        out_specs=[big, big, pl.BlockSpec((PEER_HEADS, tt), lambda i: (0, i))],
        out_shape=[jax.ShapeDtypeStruct((n_sub, PEER_N_KEYS, t), F32),
                   jax.ShapeDtypeStruct((n_sub, PEER_N_KEYS, t), F32),
                   jax.ShapeDtypeStruct((PEER_HEADS, t), F32)],
        scratch_shapes=[pltpu.VMEM((wqt.shape[0], tt), BF16),
                        pltpu.VMEM((PEER_N_KEYS, tt), F32),
                        pltpu.VMEM((n_sub, PEER_TOPK, tt), F32),
                        pltpu.VMEM((PEER_TOPK * PEER_TOPK, tt), F32)],
        compiler_params=_cp(("parallel",)),
        name="peer_route",
    )(h2t, wqt, keys)


def _gelu_exact(x):
    return 0.5 * x * (1.0 + lax.erf(x * (2.0 ** -0.5)))


def _peer_dense_kernel(h2t_ref, u_ref, vt_ref, s_ref, e_ref, tau_ref, o_ref, g_s, *, rows_per_tile):
    ei = pl.program_id(1)

    @pl.when(ei == 0)
    def _():
        o_ref[...] = jnp.zeros_like(o_ref)

    act = _gelu_exact(jnp.dot(u_ref[...], h2t_ref[...], preferred_element_type=F32))
    for ii in range(rows_per_tile):
        i1 = ei * rows_per_tile + ii
        g = None
        for h in range(PEER_HEADS):
            s1 = s_ref[2 * h, pl.ds(i1, 1), :]
            e1 = e_ref[2 * h, pl.ds(i1, 1), :]
            hit = (s1 + s_ref[2 * h + 1]) >= tau_ref[h:h + 1, :]
            term = jnp.where(hit, e1 * e_ref[2 * h + 1], 0.0)
            g = term if g is None else g + term
        sl = slice(ii * PEER_N_KEYS, (ii + 1) * PEER_N_KEYS)
        g_s[sl, :] = (g * act[sl, :]).astype(BF16)
    o_ref[...] += jnp.dot(vt_ref[...], g_s[...], preferred_element_type=F32)


def _peer_dense(h2t, u, vt, s, e, tau):
    t = h2t.shape[1]
    n_exp = u.shape[0]
    tt = 512
    rows_per_tile = 4
    te = rows_per_tile * PEER_N_KEYS
    n_sub = 2 * PEER_HEADS
    big = pl.BlockSpec((n_sub, PEER_N_KEYS, tt), lambda i, j: (0, 0, i))
    return pl.pallas_call(
        functools.partial(_peer_dense_kernel, rows_per_tile=rows_per_tile),
        grid=(t // tt, n_exp // te),
        in_specs=[pl.BlockSpec((D_MODEL, tt), lambda i, j: (0, i)),
                  pl.BlockSpec((te, D_MODEL), lambda i, j: (j, 0)),
                  pl.BlockSpec((D_MODEL, te), lambda i, j: (0, j)),
                  big, big,
                  pl.BlockSpec((PEER_HEADS, tt), lambda i, j: (0, i))],
        out_specs=pl.BlockSpec((D_MODEL, tt), lambda i, j: (0, i)),
        out_shape=jax.ShapeDtypeStruct((D_MODEL, t), F32),
        scratch_shapes=[pltpu.VMEM((te, tt), BF16)],
        compiler_params=_cp(("parallel", "arbitrary")),
        name="peer_dense",
    )(h2t, u, vt, s, e, tau)


def _final_kernel(x_ref, p_ref, g2_ref, fg_ref, o_ref):
    x = x_ref[...] + g2_ref[...] * p_ref[...]
    o_ref[...] = x * lax.rsqrt(jnp.mean(x * x, axis=-1, keepdims=True) + RMS_EPS) * fg_ref[...]


def _final(x, p, mod3, gate_chunk, final_g, n_ctx_rows):
    t = x.shape[0]
    tm = 256
    row = functools.partial(_mod_row, tm=tm, n_ctx_rows=n_ctx_rows)
    tok = pl.BlockSpec((tm, D_MODEL), lambda i: (i, 0))
    return pl.pallas_call(
        _final_kernel,
        grid=(t // tm,),
        in_specs=[tok, tok,
                  pl.BlockSpec((None, 1, D_MODEL), lambda i: (row(i), 0, gate_chunk)),
                  pl.BlockSpec((1, D_MODEL), lambda i: (0, 0))],
        out_specs=tok,
        out_shape=jax.ShapeDtypeStruct((t, D_MODEL), F32),
        compiler_params=_cp(("parallel",)),
        name="final_norm",
    )(x, p, mod3, final_g)


def _group_heads(p):
    return jnp.transpose(p.reshape(2, SSD_GROUPS, HEADS_PER_GROUP), (1, 0, 2)).reshape(
        SSD_GROUPS, 2 * HEADS_PER_GROUP)


def _layer(x, lp, mod3, n_ctx_batch, n_lat_batch, cache):
    n_ctx_rows = n_ctx_batch * SEQ
    t = x.shape[0]
    w_in = lp['w_in']
    na_w = NA_HEADS * NA_HEAD_DIM
    o_z = 3 * na_w
    o_xbc = o_z + SSD_D_INNER
    o_dt = o_xbc + SSD_CONV_CH
    o_g = o_dt + 2 * SSD_HEADS

    h1 = _norm_mod(x, lp['norm1_g'][None, :], mod3, 1, 0, n_ctx_rows)
    q = _mm(h1, w_in[:, 0:na_w].astype(BF16), BF16, name="in_q")
    k = _mm(h1, w_in[:, na_w:2 * na_w].astype(BF16), F32, name="in_k")
    v = _mm(h1, w_in[:, 2 * na_w:3 * na_w].astype(BF16), F32, name="in_v")
    z = _mm(h1, w_in[:, o_z:o_xbc].astype(BF16), F32, name="in_z")
    xbc = _mm(h1, w_in[:, o_xbc:o_dt].astype(BF16), F32, name="in_xbc")
    dt_raw = _mm(h1, w_in[:, o_dt:o_g].astype(BF16), F32, name="in_dt")
    gates = _mm(h1, w_in[:, o_g:].astype(BF16), F32, name="in_gates")

    k_ctx, v_ctx, init_f, init_b = cache
    na_ctx = _ctx_attention(q, k, v, n_ctx_batch)
    bias = _na_bias_table(lp['na_rpb'])
    na_lat = _na_latent(q, k, v, k_ctx.reshape(n_lat_batch, -1, na_w), v_ctx.reshape(n_lat_batch, -1, na_w),
                        bias, n_ctx_rows, n_lat_batch)
    na = jnp.concatenate([na_ctx, na_lat], axis=0)

    dtg = jnp.transpose(dt_raw.reshape(t, 2, SSD_GROUPS, HEADS_PER_GROUP), (2, 0, 1, 3)).reshape(
        SSD_GROUPS, t, 2 * HEADS_PER_GROUP)
    dtgt = jnp.transpose(dtg, (0, 2, 1))
    conv_w8 = jnp.concatenate([lp['conv_w'], jnp.zeros((8 - SSD_CONV, SSD_CONV_CH), F32)], axis=0)
    dtb = _group_heads(lp['ssd_dt_bias'])
    alog = _group_heads(lp['ssd_a_log'])
    dexp = jnp.repeat(lp['ssd_d'], SSD_HEAD_DIM)[None, :]
    ssd_args = (xbc, z, dtg, dtgt, conv_w8, lp['conv_b'][None, :], dtb[:, None, :], dtb[:, :, None],
                alog[:, None, :], alog[:, :, None], dexp, lp['ssd_norm_g'][None, :])
    ssd_ctx, s_f, s_b = _ssd(*ssd_args, row_start=0, nb=n_ctx_batch, seq=SEQ)
    ssd_lat = _ssd(*ssd_args, row_start=n_ctx_rows, nb=n_lat_batch, seq=DEC_SEQ, init=(init_f, init_b))
    ssd = jnp.concatenate([ssd_ctx, ssd_lat], axis=0)

    mixed = _mix(na, ssd, lp['w_na_proj'].astype(BF16), lp['w_ssd_proj'].astype(BF16), gates)
    x1 = _outproj_residual(mixed, lp['w_out'].astype(BF16), x, mod3, 2, n_ctx_rows)

    h2 = _norm_mod(x1, lp['norm2_g'][None, :], mod3, 4, 3, n_ctx_rows)
    h2t = h2.T
    keys = lp['peer_keys'].reshape(2 * PEER_HEADS, PEER_N_KEYS, PEER_HALF).astype(BF16)
    s, e, tau = _peer_route(h2t, lp['peer_wq'].T.astype(BF16), keys)
    peer_t = _peer_dense(h2t, lp['peer_u'].astype(BF16), lp['peer_v'].T.astype(BF16), s, e, tau)
    new_k = k[:n_ctx_rows].reshape(n_ctx_batch, SEQ, NA_HEADS, NA_HEAD_DIM)
    new_v = v[:n_ctx_rows].reshape(n_ctx_batch, SEQ, NA_HEADS, NA_HEAD_DIM)
    return x1, peer_t.T, (new_k, new_v, s_f, s_b)


def kernel(x_prompt, x_sample, c, c_ctx, cache_na_k, cache_na_v, state_ssd_fwd, state_ssd_bwd, ada_w, ada_b, norm1_g, norm2_g, w_in, conv_w, conv_b, na_rpb, ssd_a_log, ssd_dt_bias, ssd_d, ssd_norm_g, w_na_proj, w_ssd_proj, w_out, peer_wq, peer_keys, peer_u, peer_v, final_g):
    n_ctx_batch = x_prompt.shape[0]
    n_lat_batch = x_sample.shape[0]
    depth = ada_w.shape[0]
    n_ctx_rows = n_ctx_batch * SEQ
    assert x_prompt.shape[1] == SEQ and x_sample.shape[1] == DEC_SEQ
    assert 1 + n_lat_batch <= MOD_ROWS and n_ctx_rows % DEC_SEQ == 0

    x = jnp.concatenate([x_prompt.reshape(-1, D_MODEL), x_sample.reshape(-1, D_MODEL)], axis=0)
    cvec = jnp.concatenate([c_ctx[None, :], c, jnp.zeros((MOD_ROWS - 1 - n_lat_batch, D_MODEL), F32)], axis=0)
    new_k, new_v, new_f, new_b = [], [], [], []
    peer_out = None
    mod3 = None
    for i in range(depth):
        if peer_out is not None:
            x = x + _gate_rows(mod3, 5, n_ctx_rows, n_lat_batch) * peer_out
        lp = {'norm1_g': norm1_g[i], 'norm2_g': norm2_g[i], 'w_in': w_in[i], 'conv_w': conv_w[i],
              'conv_b': conv_b[i], 'na_rpb': na_rpb[i], 'ssd_a_log': ssd_a_log[i],
              'ssd_dt_bias': ssd_dt_bias[i], 'ssd_d': ssd_d[i], 'ssd_norm_g': ssd_norm_g[i],
              'w_na_proj': w_na_proj[i], 'w_ssd_proj': w_ssd_proj[i], 'w_out': w_out[i],
              'peer_wq': peer_wq[i], 'peer_keys': peer_keys[i], 'peer_u': peer_u[i], 'peer_v': peer_v[i]}
        mod3 = _ada(cvec, ada_w[i], ada_b[i][None, :])[:, None, :]
        cache = (cache_na_k[:, i], cache_na_v[:, i], state_ssd_fwd[:, i], state_ssd_bwd[:, i])
        x, peer_out, (k_i, v_i, sf_i, sb_i) = _layer(x, lp, mod3, n_ctx_batch, n_lat_batch, cache)
        new_k.append(k_i)
        new_v.append(v_i)
        new_f.append(sf_i)
        new_b.append(sb_i)
    y = _final(x, peer_out, mod3, 5, final_g[None, :], n_ctx_rows)
    y_prompt = y[:n_ctx_rows].reshape(x_prompt.shape)
    y_sample = y[n_ctx_rows:].reshape(x_sample.shape)
    return (y_prompt, y_sample, jnp.stack(new_k, axis=1), jnp.stack(new_v, axis=1),
            jnp.stack(new_f, axis=1), jnp.stack(new_b, axis=1))


def _gate_rows(mod3, chunk, n_ctx_rows, n_lat_batch):
    g = mod3[:, 0, chunk * D_MODEL:(chunk + 1) * D_MODEL]
    return jnp.concatenate([jnp.broadcast_to(g[0:1], (n_ctx_rows, D_MODEL)),
                            jnp.repeat(g[1:1 + n_lat_batch], DEC_SEQ, axis=0)], axis=0)
```

```python
import functools

import jax
import jax.numpy as jnp
from jax import lax
from jax.experimental import pallas as pl
from jax.experimental.pallas import tpu as pltpu

F32 = jnp.float32
BF16 = jnp.bfloat16

D_MODEL = 2048
SEQ = 256
DEC_SEQ = 1024
GRID_W = 64
NA_HEADS = 16
NA_HEAD_DIM = 128
NA_WIN_ROWS = 8
NA_WIN_COLS = 16
SSD_D_INNER = 4096
SSD_HEAD_DIM = 64
SSD_HEADS = 64
SSD_GROUPS = 8
SSD_STATE = 128
SSD_CONV = 5
SSD_CHUNK = 128
SSD_CONV_CH = SSD_D_INNER + 2 * SSD_GROUPS * SSD_STATE
HEADS_PER_GROUP = SSD_HEADS // SSD_GROUPS
GROUP_CH = SSD_D_INNER // SSD_GROUPS
PEER_HEADS = 8
PEER_HALF = 128
PEER_N_KEYS = 128
PEER_TOPK = 16
RMS_EPS = 1e-6
NEG_BIG = -1e30
MOD_ROWS = 8
VMEM_LIMIT_MB = 48

NT_DIMS = (((1,), (1,)), ((), ()))


def _cp(sem, vmem_mb=VMEM_LIMIT_MB, flags=None):
    return pltpu.CompilerParams(dimension_semantics=sem, vmem_limit_bytes=vmem_mb * 1024 * 1024, flags=flags)


def _mod_row(i, tm, n_ctx_rows):
    n_ctx_tiles = n_ctx_rows // tm
    return jnp.where(i < n_ctx_tiles, 0, 1 + (i - n_ctx_tiles) // (DEC_SEQ // tm))


def _silu(x):
    return x * jax.nn.sigmoid(x)


def _softplus(x):
    return jnp.maximum(x, 0.0) + jnp.log1p(jnp.exp(-jnp.abs(x)))


def _split3(x):
    hi = x.astype(BF16)
    r1 = x - hi.astype(F32)
    mid = r1.astype(BF16)
    lo = (r1 - mid.astype(F32)).astype(BF16)
    return hi, mid, lo


def _dot_f32_by_01(x, m01):
    return sum(jnp.dot(p, m01, preferred_element_type=F32) for p in _split3(x))


def _dot_01_by_f32(m01, x):
    return sum(jnp.dot(m01, p, preferred_element_type=F32) for p in _split3(x))


def _ada_kernel(c_ref, w_ref, b_ref, o_ref):
    s = _silu(c_ref[...])
    o_ref[...] = jnp.dot(s, w_ref[...], preferred_element_type=F32,
                         precision=lax.Precision.HIGHEST) + b_ref[...]


def _ada(cvec, ada_w, ada_b):
    n = ada_w.shape[1]
    tn = 1024
    return pl.pallas_call(
        _ada_kernel,
        grid=(n // tn,),
        in_specs=[pl.BlockSpec((MOD_ROWS, D_MODEL), lambda j: (0, 0)),
                  pl.BlockSpec((D_MODEL, tn), lambda j: (0, j)),
                  pl.BlockSpec((1, tn), lambda j: (0, j))],
        out_specs=pl.BlockSpec((MOD_ROWS, tn), lambda j: (0, j)),
        out_shape=jax.ShapeDtypeStruct((MOD_ROWS, n), F32),
        compiler_params=_cp(("parallel",)),
        name="ada_mod",
    )(cvec, ada_w, ada_b)


def _norm_mod_kernel(x_ref, g_ref, sc_ref, sh_ref, o_ref):
    x = x_ref[...]
    y = x * lax.rsqrt(jnp.mean(x * x, axis=-1, keepdims=True) + RMS_EPS) * g_ref[...]
    o_ref[...] = (y * (1.0 + sc_ref[...]) + sh_ref[...]).astype(o_ref.dtype)


def _norm_mod(x, g, mod3, sc_chunk, sh_chunk, n_ctx_rows):
    t = x.shape[0]
    tm = 256
    row = functools.partial(_mod_row, tm=tm, n_ctx_rows=n_ctx_rows)
    return pl.pallas_call(
        _norm_mod_kernel,
        grid=(t // tm,),
        in_specs=[pl.BlockSpec((tm, D_MODEL), lambda i: (i, 0)),
                  pl.BlockSpec((1, D_MODEL), lambda i: (0, 0)),
                  pl.BlockSpec((None, 1, D_MODEL), lambda i: (row(i), 0, sc_chunk)),
                  pl.BlockSpec((None, 1, D_MODEL), lambda i: (row(i), 0, sh_chunk))],
        out_specs=pl.BlockSpec((tm, D_MODEL), lambda i: (i, 0)),
        out_shape=jax.ShapeDtypeStruct((t, D_MODEL), BF16),
        compiler_params=_cp(("parallel",)),
        name="norm_mod",
    )(x, g, mod3, mod3)


def _mm_kernel(a_ref, b_ref, o_ref):
    o_ref[...] = jnp.dot(a_ref[...], b_ref[...], preferred_element_type=F32).astype(o_ref.dtype)


def _mm(a, b, out_dtype, tm=1024, tn=1024, name="mm"):
    m, k = a.shape
    n = b.shape[1]
    tm = min(tm, m)
    tn = min(tn, n)
    return pl.pallas_call(
        _mm_kernel,
        grid=(m // tm, n // tn),
        in_specs=[pl.BlockSpec((tm, k), lambda i, j: (i, 0)),
                  pl.BlockSpec((k, tn), lambda i, j: (0, j))],
        out_specs=pl.BlockSpec((tm, tn), lambda i, j: (i, j)),
        out_shape=jax.ShapeDtypeStruct((m, n), out_dtype),
        compiler_params=_cp(("parallel", "parallel")),
        name=name,
    )(a, b)


def _ctx_attn_kernel(q_ref, k_ref, v_ref, o_ref):
    scale = NA_HEAD_DIM ** -0.5
    for h in range(NA_HEADS):
        sl = slice(h * NA_HEAD_DIM, (h + 1) * NA_HEAD_DIM)
        q = q_ref[:, sl]
        k = k_ref[:, sl].astype(BF16)
        v = v_ref[:, sl].astype(BF16)
        s = lax.dot_general(q, k, NT_DIMS, preferred_element_type=F32) * scale
        p = jnp.exp(s - jnp.max(s, axis=-1, keepdims=True))
        l = jnp.sum(p, axis=-1, keepdims=True)
        o = jnp.dot(p.astype(BF16), v, preferred_element_type=F32) / l
        o_ref[:, sl] = o.astype(o_ref.dtype)


def _ctx_attention(q, k, v, n_ctx_batch):
    width = NA_HEADS * NA_HEAD_DIM
    spec = pl.BlockSpec((SEQ, width), lambda b: (b, 0))
    return pl.pallas_call(
        _ctx_attn_kernel,
        grid=(n_ctx_batch,),
        in_specs=[spec, spec, spec],
        out_specs=spec,
        out_shape=jax.ShapeDtypeStruct((n_ctx_batch * SEQ, width), BF16),
        compiler_params=_cp(("parallel",)),
        name="ctx_attn",
    )(q, k, v)


def _na_lat_kernel(q_ref, k_ref, v_ref, kc_ref, vc_ref, rpb_ref, o_ref, pair_s):
    scale = NA_HEAD_DIM ** -0.5
    rows = DEC_SEQ // GRID_W
    kc = kc_ref[...].astype(BF16)
    vc = vc_ref[...].astype(BF16)

    two_w = 2 * GRID_W
    qi = lax.broadcasted_iota(jnp.int32, (GRID_W, two_w), 0)
    lane = lax.broadcasted_iota(jnp.int32, (GRID_W, two_w), 1)
    kj = lane & (GRID_W - 1)
    col_start = jnp.clip(qi - NA_WIN_COLS // 2, 0, GRID_W - NA_WIN_COLS)
    in_window = (kj >= col_start) & (kj < col_start + NA_WIN_COLS)
    first_half = lane < GRID_W
    n_off = 2 * NA_WIN_ROWS - 1

    def toeplitz(d, lane_off):
        row = jnp.broadcast_to(rpb_ref[d:d + 1, :], (GRID_W, two_w))
        return pltpu.roll(row, (lane_off - (NA_WIN_COLS - 1)) % two_w, 1, stride=1, stride_axis=0)

    for d in range(n_off - 1):
        both = jnp.where(first_half, toeplitz(d, 0), toeplitz(d + 1, GRID_W))
        pair_s[d] = jnp.where(in_window, both, NEG_BIG)

    for r in range(rows):
        rs = min(max(r - NA_WIN_ROWS // 2, 0), rows - NA_WIN_ROWS)
        q = q_ref[r * GRID_W:(r + 1) * GRID_W, :]
        win = slice(rs * GRID_W, (rs + NA_WIN_ROWS) * GRID_W)
        kw = k_ref[win, :].astype(BF16)
        vw = v_ref[win, :].astype(BF16)
        bias = jnp.concatenate([pair_s[rs + 2 * kp - r + NA_WIN_ROWS - 1] for kp in range(NA_WIN_ROWS // 2)],
                               axis=1)
        sw = lax.dot_general(q, kw, NT_DIMS, preferred_element_type=F32) * scale + bias
        sc = lax.dot_general(q, kc, NT_DIMS, preferred_element_type=F32) * scale
        m = jnp.maximum(jnp.max(sw, axis=-1, keepdims=True), jnp.max(sc, axis=-1, keepdims=True))
        pw = jnp.exp(sw - m)
        pc = jnp.exp(sc - m)
        l = jnp.sum(pw, axis=-1, keepdims=True) + jnp.sum(pc, axis=-1, keepdims=True)
        o = (jnp.dot(pw.astype(BF16), vw, preferred_element_type=F32)
             + jnp.dot(pc.astype(BF16), vc, preferred_element_type=F32)) / l
        o_ref[r * GRID_W:(r + 1) * GRID_W, :] = o.astype(o_ref.dtype)


def _na_latent(q, k, v, k_ctx, v_ctx, rpb, n_ctx_rows, n_lat_batch):
    off = n_ctx_rows // DEC_SEQ
    n_off = 2 * NA_WIN_ROWS - 1
    rpb_pad = jnp.zeros((NA_HEADS, 16, 2 * GRID_W), F32).at[:, :n_off, :2 * NA_WIN_COLS - 1].set(rpb)
    tok = pl.BlockSpec((DEC_SEQ, NA_HEAD_DIM), lambda b, h: (off + b, h))
    ctx = pl.BlockSpec((None, k_ctx.shape[1], NA_HEAD_DIM), lambda b, h: (b, 0, h))
    return pl.pallas_call(
        _na_lat_kernel,
        grid=(n_lat_batch, NA_HEADS),
        in_specs=[tok, tok, tok, ctx, ctx,
                  pl.BlockSpec((None, 16, 2 * GRID_W), lambda b, h: (h, 0, 0))],
        out_specs=pl.BlockSpec((DEC_SEQ, NA_HEAD_DIM), lambda b, h: (b, h)),
        out_shape=jax.ShapeDtypeStruct((n_lat_batch * DEC_SEQ, NA_HEADS * NA_HEAD_DIM), BF16),
        scratch_shapes=[pltpu.VMEM((n_off - 1, GRID_W, 2 * GRID_W), F32)],
        compiler_params=_cp(("parallel", "parallel")),
        name="na_latent",
    )(q, k, v, k_ctx, v_ctx, rpb_pad)


def _ssd_kernel(*refs, seq, has_init):
    n_in = 20 if has_init else 18
    (xs_ref, b_ref, c_ref, z_ref, dt_ref, dtt_ref, cwx_ref, cwb_ref, cwc_ref,
     cbx_ref, cbb_ref, cbc_ref, dtb_ref, dtbt_ref, al_ref, alt_ref, dexp_ref, ng_ref) = refs[:18]
    if has_init:
        if_ref, ib_ref = refs[18:20]
        (y_ref,) = refs[n_in:n_in + 1]
        scratch = refs[n_in + 1:]
    else:
        y_ref, sfo_ref, sbo_ref = refs[n_in:n_in + 3]
        scratch = refs[n_in + 3:]
    pad_s, xs_s, bb_s, cc_s, dt_s, cumc_s, cumr_s, yacc_s, sf_s, sb_s = scratch

    nc = seq // SSD_CHUNK
    ck = SSD_CHUNK
    halo = 8

    def conv_silu(dst_ref, src_ref, w_ref, bias_ref, width):
        pad_s[0:halo, 0:width] = jnp.zeros((halo, width), F32)
        pad_s[seq + halo:seq + 2 * halo, 0:width] = jnp.zeros((halo, width), F32)
        pad_s[halo:seq + halo, 0:width] = src_ref[...]
        for ci in range(nc):
            base = halo - SSD_CONV // 2 + ci * ck
            acc = bias_ref[...] + pad_s[base:base + ck, 0:width] * w_ref[0:1, :]
            for tap in range(1, SSD_CONV):
                acc = acc + pad_s[base + tap:base + tap + ck, 0:width] * w_ref[tap:tap + 1, :]
            dst_ref[ci * ck:(ci + 1) * ck, :] = _silu(acc)

    conv_silu(xs_s, xs_ref, cwx_ref, cbx_ref, GROUP_CH)
    conv_silu(bb_s, b_ref, cwb_ref, cbb_ref, SSD_STATE)
    conv_silu(cc_s, c_ref, cwc_ref, cbc_ref, SSD_STATE)

    li = lax.broadcasted_iota(jnp.int32, (ck, ck), 0)
    si = lax.broadcasted_iota(jnp.int32, (ck, ck), 1)
    lower = li >= si
    upper = si >= li
    tri_lo = jnp.where(lower, 1.0, 0.0).astype(BF16)
    tri_up = jnp.where(upper, 1.0, 0.0).astype(BF16)
    nh2 = 2 * HEADS_PER_GROUP

    dt_s[...] = _softplus(dt_ref[...] + dtb_ref[...])
    a_row = -jnp.exp(al_ref[...])
    dtt = _softplus(dtt_ref[...] + dtbt_ref[...])
    dat = dtt * (-jnp.exp(alt_ref[...]))
    col_is_fwd = lax.broadcasted_iota(jnp.int32, (ck, nh2), 1) < HEADS_PER_GROUP
    row_is_fwd = lax.broadcasted_iota(jnp.int32, (nh2, ck), 0) < HEADS_PER_GROUP
    for c in range(nc):
        da = dt_s[c * ck:(c + 1) * ck, :] * a_row
        cumc_s[c] = jnp.where(col_is_fwd, _dot_01_by_f32(tri_lo, da), _dot_01_by_f32(tri_up, da))
        dat_c = dat[:, c * ck:(c + 1) * ck]
        cumr_s[c] = jnp.where(row_is_fwd, _dot_f32_by_01(dat_c, tri_up), _dot_f32_by_01(dat_c, tri_lo))

    erow = lax.broadcasted_iota(jnp.int32, (nh2, GROUP_CH), 0)
    ecol = lax.broadcasted_iota(jnp.int32, (nh2, GROUP_CH), 1) // SSD_HEAD_DIM
    expand = (jnp.where(erow == ecol, 1.0, 0.0).astype(BF16),
              jnp.where(erow == ecol + HEADS_PER_GROUP, 1.0, 0.0).astype(BF16))
    lane_lo = lax.broadcasted_iota(jnp.int32, (ck, 2 * SSD_HEAD_DIM), 1) < SSD_HEAD_DIM

    yacc_s[...] = jnp.zeros_like(yacc_s)
    if has_init:
        sf_s[...] = if_ref[...].reshape(GROUP_CH, SSD_STATE).T
        sb_s[...] = ib_ref[...].reshape(GROUP_CH, SSD_STATE).T
    else:
        sf_s[...] = jnp.zeros_like(sf_s)
        sb_s[...] = jnp.zeros_like(sb_s)

    def chunk_step(i, carry):
        for d in (0, 1):
            c = i if d == 0 else nc - 1 - i
            r0 = pl.multiple_of(c * ck, ck)
            xs_c = xs_s[pl.ds(r0, ck), :]
            b_c = bb_s[pl.ds(r0, ck), :]
            c_c = cc_s[pl.ds(r0, ck), :].astype(BF16)
            cumc = jnp.where(col_is_fwd if d == 0 else ~col_is_fwd, cumc_s[c], 0.0)
            cumr = cumr_s[c]
            e01 = expand[d]
            state_ref = sf_s if d == 0 else sb_s
            tri_mask = lower if d == 0 else upper

            cb = lax.dot_general(c_c, b_c.astype(BF16), NT_DIMS, preferred_element_type=F32)
            xdt = xs_c * _dot_f32_by_01(dt_s[pl.ds(r0, ck), :], e01)
            xdt_bf = xdt.astype(BF16)
            total = cumc[ck - 1:ck, :] if d == 0 else cumc[0:1, :]
            eacs_x = _dot_f32_by_01(jnp.exp(cumc), e01)
            decay_x = _dot_f32_by_01(jnp.exp(total - cumc), e01)
            dtot_x = _dot_f32_by_01(jnp.broadcast_to(jnp.exp(total), (8, nh2)), e01)[0:1, :]

            state = state_ref[...]
            y = jnp.dot(c_c, state.astype(BF16), preferred_element_type=F32) * eacs_x
            state_ref[...] = state * dtot_x + jnp.dot(
                b_c.T.astype(BF16), (decay_x * xdt).astype(BF16), preferred_element_type=F32)

            pieces = []
            for hp in range(HEADS_PER_GROUP // 2):
                x_pair = xdt_bf[:, hp * 2 * SSD_HEAD_DIM:(hp + 1) * 2 * SSD_HEAD_DIM]
                outs = []
                for sub in range(2):
                    j = d * HEADS_PER_GROUP + hp * 2 + sub
                    seg = cumc[:, j:j + 1] - cumr[j:j + 1, :]
                    lmat = jnp.where(tri_mask, jnp.exp(jnp.where(tri_mask, seg, 0.0)), 0.0)
                    outs.append(jnp.dot((cb * lmat).astype(BF16), x_pair, preferred_element_type=F32))
                pieces.append(jnp.where(lane_lo, outs[0], outs[1]))
            y = y + jnp.concatenate(pieces, axis=1)
            yacc_s[pl.ds(r0, ck), :] = yacc_s[pl.ds(r0, ck), :] + y
        return carry

    lax.fori_loop(0, nc, chunk_step, 0)

    for ci in range(nc):
        rows = slice(ci * ck, (ci + 1) * ck)
        zz = z_ref[rows, :]
        y = (yacc_s[rows, :] + dexp_ref[...] * xs_s[rows, :]) * _silu(zz)
        y = y * lax.rsqrt(jnp.mean(y * y, axis=-1, keepdims=True) + RMS_EPS) * ng_ref[...]
        y_ref[rows, :] = y.astype(y_ref.dtype)

    if not has_init:
        sfo_ref[...] = sf_s[...].T.reshape(HEADS_PER_GROUP, SSD_HEAD_DIM, SSD_STATE)
        sbo_ref[...] = sb_s[...].T.reshape(HEADS_PER_GROUP, SSD_HEAD_DIM, SSD_STATE)


def _ssd(xbc, z, dtg, dtgt, conv_w8, conv_b, dtb, dtbt, alog, alogt, dexp, norm_g,
         row_start, nb, seq, init=None):
    off = row_start // seq
    g_b = SSD_D_INNER // SSD_STATE
    g_c = g_b + SSD_GROUPS
    has_init = init is not None
    in_specs = [
        pl.BlockSpec((seq, GROUP_CH), lambda b, g: (off + b, g)),
        pl.BlockSpec((seq, SSD_STATE), lambda b, g: (off + b, g_b + g)),
        pl.BlockSpec((seq, SSD_STATE), lambda b, g: (off + b, g_c + g)),
        pl.BlockSpec((seq, GROUP_CH), lambda b, g: (off + b, g)),
        pl.BlockSpec((None, seq, 2 * HEADS_PER_GROUP), lambda b, g: (g, off + b, 0)),
        pl.BlockSpec((None, 2 * HEADS_PER_GROUP, seq), lambda b, g: (g, 0, off + b)),
        pl.BlockSpec((8, GROUP_CH), lambda b, g: (0, g)),
        pl.BlockSpec((8, SSD_STATE), lambda b, g: (0, g_b + g)),
        pl.BlockSpec((8, SSD_STATE), lambda b, g: (0, g_c + g)),
        pl.BlockSpec((1, GROUP_CH), lambda b, g: (0, g)),
        pl.BlockSpec((1, SSD_STATE), lambda b, g: (0, g_b + g)),
        pl.BlockSpec((1, SSD_STATE), lambda b, g: (0, g_c + g)),
        pl.BlockSpec((None, 1, 2 * HEADS_PER_GROUP), lambda b, g: (g, 0, 0)),
        pl.BlockSpec((None, 2 * HEADS_PER_GROUP, 1), lambda b, g: (g, 0, 0)),
        pl.BlockSpec((None, 1, 2 * HEADS_PER_GROUP), lambda b, g: (g, 0, 0)),
        pl.BlockSpec((None, 2 * HEADS_PER_GROUP, 1), lambda b, g: (g, 0, 0)),
        pl.BlockSpec((1, GROUP_CH), lambda b, g: (0, g)),
        pl.BlockSpec((1, GROUP_CH), lambda b, g: (0, g)),
    ]
    args = [xbc, xbc, xbc, z, dtg, dtgt, conv_w8, conv_w8, conv_w8, conv_b, conv_b, conv_b,
            dtb, dtbt, alog, alogt, dexp, norm_g]
    state_spec = pl.BlockSpec((None, HEADS_PER_GROUP, SSD_HEAD_DIM, SSD_STATE), lambda b, g: (b, g, 0, 0))
    y_spec = pl.BlockSpec((seq, GROUP_CH), lambda b, g: (b, g))
    y_shape = jax.ShapeDtypeStruct((nb * seq, SSD_D_INNER), BF16)
    if has_init:
        in_specs += [state_spec, state_spec]
        args += list(init)
        out_specs = y_spec
        out_shape = y_shape
    else:
        st_shape = jax.ShapeDtypeStruct((nb, SSD_HEADS, SSD_HEAD_DIM, SSD_STATE), F32)
        out_specs = [y_spec, state_spec, state_spec]
        out_shape = [y_shape, st_shape, st_shape]
    nc = seq // SSD_CHUNK
    scratch = [
        pltpu.VMEM((seq + 16, GROUP_CH), F32),
        pltpu.VMEM((seq, GROUP_CH), F32),
        pltpu.VMEM((seq, SSD_STATE), F32),
        pltpu.VMEM((seq, SSD_STATE), F32),
        pltpu.VMEM((seq, 2 * HEADS_PER_GROUP), F32),
        pltpu.VMEM((nc, SSD_CHUNK, 2 * HEADS_PER_GROUP), F32),
        pltpu.VMEM((nc, 2 * HEADS_PER_GROUP, SSD_CHUNK), F32),
        pltpu.VMEM((seq, GROUP_CH), F32),
        pltpu.VMEM((SSD_STATE, GROUP_CH), F32),
        pltpu.VMEM((SSD_STATE, GROUP_CH), F32),
    ]
    return pl.pallas_call(
        functools.partial(_ssd_kernel, seq=seq, has_init=has_init),
        grid=(nb, SSD_GROUPS),
        in_specs=in_specs,
        out_specs=out_specs,
        out_shape=out_shape,
        scratch_shapes=scratch,
        compiler_params=_cp(("parallel", "parallel")),
        name="ssd_lat" if has_init else "ssd_ctx",
    )(*args)


def _mix_kernel(na_ref, ssd_ref, wna_ref, wssd_ref, gna_ref, gssd_ref, o_ref):
    a = jnp.dot(na_ref[...], wna_ref[...], preferred_element_type=F32)
    s = jnp.dot(ssd_ref[...], wssd_ref[...], preferred_element_type=F32)
    o = jax.nn.sigmoid(gna_ref[...]) * a + jax.nn.sigmoid(gssd_ref[...]) * s
    o_ref[...] = o.astype(o_ref.dtype)


def _mix(na, ssd, w_na, w_ssd, gates):
    t = na.shape[0]
    tm, tn = 512, 512
    nj = D_MODEL // tn
    return pl.pallas_call(
        _mix_kernel,
        grid=(t // tm, nj),
        in_specs=[pl.BlockSpec((tm, na.shape[1]), lambda i, j: (i, 0)),
                  pl.BlockSpec((tm, ssd.shape[1]), lambda i, j: (i, 0)),
                  pl.BlockSpec((w_na.shape[0], tn), lambda i, j: (0, j)),
                  pl.BlockSpec((w_ssd.shape[0], tn), lambda i, j: (0, j)),
                  pl.BlockSpec((tm, tn), lambda i, j: (i, j)),
                  pl.BlockSpec((tm, tn), lambda i, j: (i, nj + j))],
        out_specs=pl.BlockSpec((tm, tn), lambda i, j: (i, j)),
        out_shape=jax.ShapeDtypeStruct((t, D_MODEL), BF16),
        compiler_params=_cp(("parallel", "parallel")),
        name="branch_mix",
    )(na, ssd, w_na, w_ssd, gates, gates)


def _outproj_kernel(a_ref, w_ref, x_ref, g_ref, o_ref):
    y = jnp.dot(a_ref[...], w_ref[...], preferred_element_type=F32)
    o_ref[...] = x_ref[...] + g_ref[...] * y


def _outproj_residual(a, w, x, mod3, gate_chunk, n_ctx_rows):
    t = a.shape[0]
    tm, tn = 1024, 1024
    nj = D_MODEL // tn
    row = functools.partial(_mod_row, tm=tm, n_ctx_rows=n_ctx_rows)
    return pl.pallas_call(
        _outproj_kernel,
        grid=(t // tm, nj),
        in_specs=[pl.BlockSpec((tm, a.shape[1]), lambda i, j: (i, 0)),
                  pl.BlockSpec((w.shape[0], tn), lambda i, j: (0, j)),
                  pl.BlockSpec((tm, tn), lambda i, j: (i, j)),
                  pl.BlockSpec((None, 1, tn), lambda i, j: (row(i), 0, gate_chunk * nj + j))],
        out_specs=pl.BlockSpec((tm, tn), lambda i, j: (i, j)),
        out_shape=jax.ShapeDtypeStruct((t, D_MODEL), F32),
        compiler_params=_cp(("parallel", "parallel")),
        name="out_proj",
    )(a, w, x, mod3)


def _peer_route_kernel(h2t_ref, wqt_ref, keys_ref, s_ref, e_ref, tau_ref, qt_s, slab_s, sv_s, cand_s):
    tt = h2t_ref.shape[1]
    n_sub = 2 * PEER_HEADS
    kk = PEER_TOPK
    qt_s[...] = jnp.dot(wqt_ref[...], h2t_ref[...], preferred_element_type=F32).astype(BF16)

    def half_step(c, carry):
        r0 = pl.multiple_of(c * PEER_HALF, PEER_HALF)
        sc = jnp.dot(keys_ref[c], qt_s[pl.ds(r0, PEER_HALF), :], preferred_element_type=F32)
        s_ref[c] = sc
        slab_s[...] = sc

        def extract(r, inner):
            cur = slab_s[...]
            m = jnp.max(cur, axis=0, keepdims=True)
            sv_s[c, pl.ds(r, 1), :] = m
            slab_s[...] = jnp.where(cur == m, -jnp.inf, cur)
            return inner

        lax.fori_loop(0, kk, extract, 0)
        return carry

    lax.fori_loop(0, n_sub, half_step, 0)

    def head_step(h, carry):
        sv1 = sv_s[2 * h]
        sv2 = sv_s[2 * h + 1]
        for a in range(kk):
            cand_s[a * kk:(a + 1) * kk, :] = sv1[a:a + 1, :] + sv2
        m0 = sv1[0:1, :] + sv2[0:1, :]

        def extract(r, st):
            taken, tau, zsum = st
            cur = cand_s[...]
            m = jnp.max(cur, axis=0, keepdims=True)
            eq = cur == m
            cnt = jnp.sum(jnp.where(eq, 1.0, 0.0), axis=0, keepdims=True)
            active = taken < kk
            tau = jnp.where(active, m, tau)
            zsum = zsum + jnp.where(active, cnt * jnp.exp(m - m0), 0.0)
            taken = taken + jnp.where(active, cnt, 0.0)
            cand_s[...] = jnp.where(eq, -jnp.inf, cur)
            return taken, tau, zsum

        zero = jnp.zeros((1, tt), F32)
        _, tau, zsum = lax.fori_loop(0, kk, extract, (zero, m0, zero))
        tau_ref[pl.ds(h, 1), :] = tau
        e_ref[2 * h] = jnp.exp(s_ref[2 * h] - sv1[0:1, :]) / zsum
        e_ref[2 * h + 1] = jnp.exp(s_ref[2 * h + 1] - sv2[0:1, :])
        return carry

    lax.fori_loop(0, PEER_HEADS, head_step, 0)


def _peer_route(h2t, wqt, keys):
    t = h2t.shape[1]
    tt = 512
    n_sub = 2 * PEER_HEADS
    big = pl.BlockSpec((n_sub, PEER_N_KEYS, tt), lambda i: (0, 0, i))
    return pl.pallas_call(
        _peer_route_kernel,
        grid=(t // tt,),
        in_specs=[pl.BlockSpec((D_MODEL, tt), lambda i: (0, i)),
                  pl.BlockSpec(wqt.shape, lambda i: (0, 0)),
                  pl.BlockSpec(keys.shape, lambda i: (0, 0, 0))],
        out_specs=[big, big, pl.BlockSpec((PEER_HEADS, tt), lambda i: (0, i))],
        out_shape=[jax.ShapeDtypeStruct((n_sub, PEER_N_KEYS, t), F32),
                   jax.ShapeDtypeStruct((n_sub, PEER_N_KEYS, t), F32),
                   jax.ShapeDtypeStruct((PEER_HEADS, t), F32)],
        scratch_shapes=[pltpu.VMEM((wqt.shape[0], tt), BF16),
                        pltpu.VMEM((PEER_N_KEYS, tt), F32),
                        pltpu.VMEM((n_sub, PEER_TOPK, tt), F32),
                        pltpu.VMEM((PEER_TOPK * PEER_TOPK, tt), F32)],
        compiler_params=_cp(("parallel",)),
        name="peer_route",
    )(h2t, wqt, keys)


def _gelu_exact(x):
    return 0.5 * x * (1.0 + lax.erf(x * (2.0 ** -0.5)))


def _peer_dense_kernel(h2t_ref, u_ref, vt_ref, s1_ref, e1_ref, s2_ref, e2_ref, tau_ref, o_ref, g_s, act_s,
                       *, rows_per_tile):
    @pl.when(pl.program_id(1) == 0)
    def _():
        o_ref[...] = jnp.zeros_like(o_ref)

    tt = h2t_ref.shape[1]
    act_s[...] = _gelu_exact(jnp.dot(u_ref[...], h2t_ref[...], preferred_element_type=F32))
    for ii in range(rows_per_tile):
        sl = slice(ii * PEER_N_KEYS, (ii + 1) * PEER_N_KEYS)
        for lt in range(tt // 128):
            ln = slice(lt * 128, (lt + 1) * 128)
            g = None
            for h in range(PEER_HEADS):
                hit = (s1_ref[ii, h:h + 1, ln] + s2_ref[h, :, ln]) >= tau_ref[h:h + 1, ln]
                term = jnp.where(hit, e1_ref[ii, h:h + 1, ln] * e2_ref[h, :, ln], 0.0)
                g = term if g is None else g + term
            g_s[sl, ln] = (g * act_s[sl, ln]).astype(BF16)
    o_ref[...] += jnp.dot(vt_ref[...], g_s[...], preferred_element_type=F32)


def _peer_dense(h2t, u, vt, s, e, tau):
    t = h2t.shape[1]
    n_exp = u.shape[0]
    tt = 512
    rows_per_tile = 4
    te = rows_per_tile * PEER_N_KEYS
    s1 = jnp.transpose(s[0::2], (1, 0, 2))
    e1 = jnp.transpose(e[0::2], (1, 0, 2))
    rows = pl.BlockSpec((rows_per_tile, PEER_HEADS, tt), lambda i, j: (j, 0, i))
    slab = pl.BlockSpec((PEER_HEADS, PEER_N_KEYS, tt), lambda i, j: (0, 0, i))
    return pl.pallas_call(
        functools.partial(_peer_dense_kernel, rows_per_tile=rows_per_tile),
        grid=(t // tt, n_exp // te),
        in_specs=[pl.BlockSpec((D_MODEL, tt), lambda i, j: (0, i)),
                  pl.BlockSpec((te, D_MODEL), lambda i, j: (j, 0)),
                  pl.BlockSpec((D_MODEL, te), lambda i, j: (0, j)),
                  rows, rows, slab, slab,
                  pl.BlockSpec((PEER_HEADS, tt), lambda i, j: (0, i))],
        out_specs=pl.BlockSpec((D_MODEL, tt), lambda i, j: (0, i)),
        out_shape=jax.ShapeDtypeStruct((D_MODEL, t), F32),
        scratch_shapes=[pltpu.VMEM((te, tt), BF16), pltpu.VMEM((te, tt), F32)],
        compiler_params=_cp(("parallel", "arbitrary")),
        name="peer_dense",
    )(h2t, u, vt, s1, e1, s[1::2], e[1::2], tau)


def _final_kernel(x_ref, p_ref, g2_ref, fg_ref, o_ref):
    x = x_ref[...] + g2_ref[...] * p_ref[...]
    o_ref[...] = x * lax.rsqrt(jnp.mean(x * x, axis=-1, keepdims=True) + RMS_EPS) * fg_ref[...]


def _final(x, p, mod3, gate_chunk, final_g, n_ctx_rows):
    t = x.shape[0]
    tm = 256
    row = functools.partial(_mod_row, tm=tm, n_ctx_rows=n_ctx_rows)
    tok = pl.BlockSpec((tm, D_MODEL), lambda i: (i, 0))
    return pl.pallas_call(
        _final_kernel,
        grid=(t // tm,),
        in_specs=[tok, tok,
                  pl.BlockSpec((None, 1, D_MODEL), lambda i: (row(i), 0, gate_chunk)),
                  pl.BlockSpec((1, D_MODEL), lambda i: (0, 0))],
        out_specs=tok,
        out_shape=jax.ShapeDtypeStruct((t, D_MODEL), F32),
        compiler_params=_cp(("parallel",)),
        name="final_norm",
    )(x, p, mod3, final_g)


def _group_heads(p):
    return jnp.transpose(p.reshape(2, SSD_GROUPS, HEADS_PER_GROUP), (1, 0, 2)).reshape(
        SSD_GROUPS, 2 * HEADS_PER_GROUP)


def _layer(x, lp, mod3, n_ctx_batch, n_lat_batch, cache):
    n_ctx_rows = n_ctx_batch * SEQ
    t = x.shape[0]
    w_in = lp['w_in']
    na_w = NA_HEADS * NA_HEAD_DIM
    o_z = 3 * na_w
    o_xbc = o_z + SSD_D_INNER
    o_dt = o_xbc + SSD_CONV_CH
    o_g = o_dt + 2 * SSD_HEADS

    h1 = _norm_mod(x, lp['norm1_g'][None, :], mod3, 1, 0, n_ctx_rows)
    q = _mm(h1, w_in[:, 0:na_w].astype(BF16), BF16, name="in_q")
    k = _mm(h1, w_in[:, na_w:2 * na_w].astype(BF16), F32, name="in_k")
    v = _mm(h1, w_in[:, 2 * na_w:3 * na_w].astype(BF16), F32, name="in_v")
    z = _mm(h1, w_in[:, o_z:o_xbc].astype(BF16), F32, name="in_z")
    xbc = _mm(h1, w_in[:, o_xbc:o_dt].astype(BF16), F32, name="in_xbc")
    dt_raw = _mm(h1, w_in[:, o_dt:o_g].astype(BF16), F32, name="in_dt")
    gates = _mm(h1, w_in[:, o_g:].astype(BF16), F32, name="in_gates")

    k_ctx, v_ctx, init_f, init_b = cache
    na_ctx = _ctx_attention(q, k, v, n_ctx_batch)
    na_lat = _na_latent(q, k, v, k_ctx.reshape(n_lat_batch, -1, na_w), v_ctx.reshape(n_lat_batch, -1, na_w),
                        lp['na_rpb'], n_ctx_rows, n_lat_batch)
    na = jnp.concatenate([na_ctx, na_lat], axis=0)

    dtg = jnp.transpose(dt_raw.reshape(t, 2, SSD_GROUPS, HEADS_PER_GROUP), (2, 0, 1, 3)).reshape(
        SSD_GROUPS, t, 2 * HEADS_PER_GROUP)
    dtgt = jnp.transpose(dtg, (0, 2, 1))
    conv_w8 = jnp.concatenate([lp['conv_w'], jnp.zeros((8 - SSD_CONV, SSD_CONV_CH), F32)], axis=0)
    dtb = _group_heads(lp['ssd_dt_bias'])
    alog = _group_heads(lp['ssd_a_log'])
    dexp = jnp.repeat(lp['ssd_d'], SSD_HEAD_DIM)[None, :]
    ssd_args = (xbc, z, dtg, dtgt, conv_w8, lp['conv_b'][None, :], dtb[:, None, :], dtb[:, :, None],
                alog[:, None, :], alog[:, :, None], dexp, lp['ssd_norm_g'][None, :])
    ssd_ctx, s_f, s_b = _ssd(*ssd_args, row_start=0, nb=n_ctx_batch, seq=SEQ)
    ssd_lat = _ssd(*ssd_args, row_start=n_ctx_rows, nb=n_lat_batch, seq=DEC_SEQ, init=(init_f, init_b))
    ssd = jnp.concatenate([ssd_ctx, ssd_lat], axis=0)

    mixed = _mix(na, ssd, lp['w_na_proj'].astype(BF16), lp['w_ssd_proj'].astype(BF16), gates)
    x1 = _outproj_residual(mixed, lp['w_out'].astype(BF16), x, mod3, 2, n_ctx_rows)

    h2 = _norm_mod(x1, lp['norm2_g'][None, :], mod3, 4, 3, n_ctx_rows)
    h2t = h2.T
    keys = lp['peer_keys'].reshape(2 * PEER_HEADS, PEER_N_KEYS, PEER_HALF).astype(BF16)
    s, e, tau = _peer_route(h2t, lp['peer_wq'].T.astype(BF16), keys)
    peer_t = _peer_dense(h2t, lp['peer_u'].astype(BF16), lp['peer_v'].T.astype(BF16), s, e, tau)
    new_k = k[:n_ctx_rows].reshape(n_ctx_batch, SEQ, NA_HEADS, NA_HEAD_DIM)
    new_v = v[:n_ctx_rows].reshape(n_ctx_batch, SEQ, NA_HEADS, NA_HEAD_DIM)
    return x1, peer_t.T, (new_k, new_v, s_f, s_b)


def kernel(x_prompt, x_sample, c, c_ctx, cache_na_k, cache_na_v, state_ssd_fwd, state_ssd_bwd, ada_w, ada_b, norm1_g, norm2_g, w_in, conv_w, conv_b, na_rpb, ssd_a_log, ssd_dt_bias, ssd_d, ssd_norm_g, w_na_proj, w_ssd_proj, w_out, peer_wq, peer_keys, peer_u, peer_v, final_g):
    n_ctx_batch = x_prompt.shape[0]
    n_lat_batch = x_sample.shape[0]
    depth = ada_w.shape[0]
    n_ctx_rows = n_ctx_batch * SEQ
    assert x_prompt.shape[1] == SEQ and x_sample.shape[1] == DEC_SEQ
    assert 1 + n_lat_batch <= MOD_ROWS and n_ctx_rows % DEC_SEQ == 0

    x = jnp.concatenate([x_prompt.reshape(-1, D_MODEL), x_sample.reshape(-1, D_MODEL)], axis=0)
    cvec = jnp.concatenate([c_ctx[None, :], c, jnp.zeros((MOD_ROWS - 1 - n_lat_batch, D_MODEL), F32)], axis=0)
    new_k, new_v, new_f, new_b = [], [], [], []
    peer_out = None
    mod3 = None
    for i in range(depth):
        if peer_out is not None:
            x = x + _gate_rows(mod3, 5, n_ctx_rows, n_lat_batch) * peer_out
        lp = {'norm1_g': norm1_g[i], 'norm2_g': norm2_g[i], 'w_in': w_in[i], 'conv_w': conv_w[i],
              'conv_b': conv_b[i], 'na_rpb': na_rpb[i], 'ssd_a_log': ssd_a_log[i],
              'ssd_dt_bias': ssd_dt_bias[i], 'ssd_d': ssd_d[i], 'ssd_norm_g': ssd_norm_g[i],
              'w_na_proj': w_na_proj[i], 'w_ssd_proj': w_ssd_proj[i], 'w_out': w_out[i],
              'peer_wq': peer_wq[i], 'peer_keys': peer_keys[i], 'peer_u': peer_u[i], 'peer_v': peer_v[i]}
        mod3 = _ada(cvec, ada_w[i], ada_b[i][None, :])[:, None, :]
        cache = (cache_na_k[:, i], cache_na_v[:, i], state_ssd_fwd[:, i], state_ssd_bwd[:, i])
        x, peer_out, (k_i, v_i, sf_i, sb_i) = _layer(x, lp, mod3, n_ctx_batch, n_lat_batch, cache)
        new_k.append(k_i)
        new_v.append(v_i)
        new_f.append(sf_i)
        new_b.append(sb_i)
    y = _final(x, peer_out, mod3, 5, final_g[None, :], n_ctx_rows)
    y_prompt = y[:n_ctx_rows].reshape(x_prompt.shape)
    y_sample = y[n_ctx_rows:].reshape(x_sample.shape)
    return (y_prompt, y_sample, jnp.stack(new_k, axis=1), jnp.stack(new_v, axis=1),
            jnp.stack(new_f, axis=1), jnp.stack(new_b, axis=1))


def _gate_rows(mod3, chunk, n_ctx_rows, n_lat_batch):
    g = mod3[:, 0, chunk * D_MODEL:(chunk + 1) * D_MODEL]
    return jnp.concatenate([jnp.broadcast_to(g[0:1], (n_ctx_rows, D_MODEL)),
                            jnp.repeat(g[1:1 + n_lat_batch], DEC_SEQ, axis=0)], axis=0)
```

```python
import functools

import jax
import jax.numpy as jnp
from jax import lax
from jax.experimental import pallas as pl
from jax.experimental.pallas import tpu as pltpu

F32 = jnp.float32
BF16 = jnp.bfloat16

D_MODEL = 2048
SEQ = 256
DEC_SEQ = 1024
GRID_W = 64
NA_HEADS = 16
NA_HEAD_DIM = 128
NA_WIN_ROWS = 8
NA_WIN_COLS = 16
SSD_D_INNER = 4096
SSD_HEAD_DIM = 64
SSD_HEADS = 64
SSD_GROUPS = 8
SSD_STATE = 128
SSD_CONV = 5
SSD_CHUNK = 128
SSD_CONV_CH = SSD_D_INNER + 2 * SSD_GROUPS * SSD_STATE
HEADS_PER_GROUP = SSD_HEADS // SSD_GROUPS
GROUP_CH = SSD_D_INNER // SSD_GROUPS
PEER_HEADS = 8
PEER_HALF = 128
PEER_N_KEYS = 128
PEER_TOPK = 16
RMS_EPS = 1e-6
NEG_BIG = -1e30
MOD_ROWS = 8
VMEM_LIMIT_MB = 48

NT_DIMS = (((1,), (1,)), ((), ()))


def _cp(sem, vmem_mb=VMEM_LIMIT_MB, flags=None):
    return pltpu.CompilerParams(dimension_semantics=sem, vmem_limit_bytes=vmem_mb * 1024 * 1024, flags=flags)


def _mod_row(i, tm, n_ctx_rows):
    n_ctx_tiles = n_ctx_rows // tm
    return jnp.where(i < n_ctx_tiles, 0, 1 + (i - n_ctx_tiles) // (DEC_SEQ // tm))


def _silu(x):
    return x * jax.nn.sigmoid(x)


def _softplus(x):
    return jnp.maximum(x, 0.0) + jnp.log1p(jnp.exp(-jnp.abs(x)))


def _split3(x):
    hi = x.astype(BF16)
    r1 = x - hi.astype(F32)
    mid = r1.astype(BF16)
    lo = (r1 - mid.astype(F32)).astype(BF16)
    return hi, mid, lo


def _dot_f32_by_01(x, m01):
    return sum(jnp.dot(p, m01, preferred_element_type=F32) for p in _split3(x))


def _dot_01_by_f32(m01, x):
    return sum(jnp.dot(m01, p, preferred_element_type=F32) for p in _split3(x))


def _ada_kernel(c_ref, w_ref, b_ref, o_ref):
    s = _silu(c_ref[...])
    o_ref[...] = jnp.dot(s, w_ref[...], preferred_element_type=F32,
                         precision=lax.Precision.HIGHEST) + b_ref[...]


def _ada(cvec, ada_w, ada_b):
    n = ada_w.shape[1]
    tn = 1024
    return pl.pallas_call(
        _ada_kernel,
        grid=(n // tn,),
        in_specs=[pl.BlockSpec((MOD_ROWS, D_MODEL), lambda j: (0, 0)),
                  pl.BlockSpec((D_MODEL, tn), lambda j: (0, j)),
                  pl.BlockSpec((1, tn), lambda j: (0, j))],
        out_specs=pl.BlockSpec((MOD_ROWS, tn), lambda j: (0, j)),
        out_shape=jax.ShapeDtypeStruct((MOD_ROWS, n), F32),
        compiler_params=_cp(("parallel",)),
        name="ada_mod",
    )(cvec, ada_w, ada_b)


def _norm_mod_kernel(*refs, transpose_out, n_ctx_tiles):
    if n_ctx_tiles is None:
        x_ref, g_ref, sc_ref, sh_ref, o_ref = refs
        x = x_ref[...]
    else:
        xc_ref, xl_ref, g_ref, sc_ref, sh_ref, o_ref = refs
        x = jnp.where(pl.program_id(0) < n_ctx_tiles, xc_ref[...], xl_ref[...])
    y = x * lax.rsqrt(jnp.mean(x * x, axis=-1, keepdims=True) + RMS_EPS) * g_ref[...]
    y = y * (1.0 + sc_ref[...]) + sh_ref[...]
    o_ref[...] = (y.T if transpose_out else y).astype(o_ref.dtype)


def _norm_mod(x_parts, g, mod3, sc_chunk, sh_chunk, n_ctx_rows, transpose_out=False):
    t = sum(p.shape[0] for p in x_parts)
    tm = 256
    nct = n_ctx_rows // tm
    row = functools.partial(_mod_row, tm=tm, n_ctx_rows=n_ctx_rows)
    if len(x_parts) == 1:
        x_specs = [pl.BlockSpec((tm, D_MODEL), lambda i: (i, 0))]
    else:
        x_specs = [pl.BlockSpec((tm, D_MODEL), lambda i: (_ctx_tile(i, nct), 0)),
                   pl.BlockSpec((tm, D_MODEL), lambda i: (_lat_tile(i, nct), 0))]
    if transpose_out:
        out_spec = pl.BlockSpec((D_MODEL, tm), lambda i: (0, i))
        out_shape = jax.ShapeDtypeStruct((D_MODEL, t), BF16)
    else:
        out_spec = pl.BlockSpec((tm, D_MODEL), lambda i: (i, 0))
        out_shape = jax.ShapeDtypeStruct((t, D_MODEL), BF16)
    return pl.pallas_call(
        functools.partial(_norm_mod_kernel, transpose_out=transpose_out,
                          n_ctx_tiles=None if len(x_parts) == 1 else nct),
        grid=(t // tm,),
        in_specs=x_specs + [pl.BlockSpec((1, D_MODEL), lambda i: (0, 0)),
                            pl.BlockSpec((None, 1, D_MODEL), lambda i: (row(i), 0, sc_chunk)),
                            pl.BlockSpec((None, 1, D_MODEL), lambda i: (row(i), 0, sh_chunk))],
        out_specs=out_spec,
        out_shape=out_shape,
        compiler_params=_cp(("parallel",)),
        name="norm_mod",
    )(*x_parts, g, mod3, mod3)


def _mm_kernel(a_ref, b_ref, o_ref):
    o_ref[...] = jnp.dot(a_ref[...], b_ref[...], preferred_element_type=F32).astype(o_ref.dtype)


def _mm(a, b, out_dtype, tm=1024, tn=1024, name="mm"):
    m, k = a.shape
    n = b.shape[1]
    tm = min(tm, m)
    tn = min(tn, n)
    return pl.pallas_call(
        _mm_kernel,
        grid=(m // tm, n // tn),
        in_specs=[pl.BlockSpec((tm, k), lambda i, j: (i, 0)),
                  pl.BlockSpec((k, tn), lambda i, j: (0, j))],
        out_specs=pl.BlockSpec((tm, tn), lambda i, j: (i, j)),
        out_shape=jax.ShapeDtypeStruct((m, n), out_dtype),
        compiler_params=_cp(("parallel", "parallel")),
        name=name,
    )(a, b)


def _ctx_tile(i, n_ctx_tiles):
    return jnp.minimum(i, n_ctx_tiles - 1)


def _lat_tile(i, n_ctx_tiles):
    return jnp.maximum(i - n_ctx_tiles, 0)


def _mm_split_kernel(a_ref, b_ref, oc_ref, ol_ref, *, n_ctx_tiles):
    y = jnp.dot(a_ref[...], b_ref[...], preferred_element_type=F32)
    ol_ref[...] = y.astype(ol_ref.dtype)

    @pl.when(pl.program_id(0) < n_ctx_tiles)
    def _():
        oc_ref[...] = y.astype(oc_ref.dtype)


def _mm_split(a, b, out_dtype, n_ctx_rows, tm=1024, tn=1024, name="mm_split"):
    m, k = a.shape
    n = b.shape[1]
    nct = n_ctx_rows // tm
    nj = n // tn
    return pl.pallas_call(
        functools.partial(_mm_split_kernel, n_ctx_tiles=nct),
        grid=(m // tm, nj),
        in_specs=[pl.BlockSpec((tm, k), lambda i, j: (i, 0)),
                  pl.BlockSpec((k, tn), lambda i, j: (0, j))],
        out_specs=[pl.BlockSpec((tm, tn), lambda i, j: (_ctx_tile(i, nct), jnp.where(i < nct, j, nj - 1))),
                   pl.BlockSpec((tm, tn), lambda i, j: (_lat_tile(i, nct), jnp.where(i < nct, 0, j)))],
        out_shape=[jax.ShapeDtypeStruct((n_ctx_rows, n), out_dtype),
                   jax.ShapeDtypeStruct((m - n_ctx_rows, n), out_dtype)],
        compiler_params=_cp(("arbitrary", "arbitrary")),
        name=name,
    )(a, b)


def _ctx_attn_kernel(q_ref, k_ref, v_ref, o_ref):
    scale = NA_HEAD_DIM ** -0.5
    for h in range(NA_HEADS):
        sl = slice(h * NA_HEAD_DIM, (h + 1) * NA_HEAD_DIM)
        q = q_ref[:, sl]
        k = k_ref[:, sl].astype(BF16)
        v = v_ref[:, sl].astype(BF16)
        s = lax.dot_general(q, k, NT_DIMS, preferred_element_type=F32) * scale
        p = jnp.exp(s - jnp.max(s, axis=-1, keepdims=True))
        l = jnp.sum(p, axis=-1, keepdims=True)
        o = jnp.dot(p.astype(BF16), v, preferred_element_type=F32) / l
        o_ref[:, sl] = o.astype(o_ref.dtype)


def _ctx_attention(q, k, v, n_ctx_batch):
    width = NA_HEADS * NA_HEAD_DIM
    spec = pl.BlockSpec((SEQ, width), lambda b: (b, 0))
    return pl.pallas_call(
        _ctx_attn_kernel,
        grid=(n_ctx_batch,),
        in_specs=[spec, spec, spec],
        out_specs=spec,
        out_shape=jax.ShapeDtypeStruct((n_ctx_batch * SEQ, width), BF16),
        compiler_params=_cp(("parallel",)),
        name="ctx_attn",
    )(q, k, v)


def _na_lat_kernel(q_ref, k_ref, v_ref, kc_ref, vc_ref, rpb_ref, o_ref, pair_s):
    scale = NA_HEAD_DIM ** -0.5
    rows = DEC_SEQ // GRID_W
    kc = kc_ref[...].astype(BF16)
    vc = vc_ref[...].astype(BF16)

    two_w = 2 * GRID_W
    qi = lax.broadcasted_iota(jnp.int32, (GRID_W, two_w), 0)
    lane = lax.broadcasted_iota(jnp.int32, (GRID_W, two_w), 1)
    kj = lane & (GRID_W - 1)
    col_start = jnp.clip(qi - NA_WIN_COLS // 2, 0, GRID_W - NA_WIN_COLS)
    in_window = (kj >= col_start) & (kj < col_start + NA_WIN_COLS)
    first_half = lane < GRID_W
    n_off = 2 * NA_WIN_ROWS - 1

    def toeplitz(d, lane_off):
        row = jnp.broadcast_to(rpb_ref[d:d + 1, :], (GRID_W, two_w))
        return pltpu.roll(row, (lane_off - (NA_WIN_COLS - 1)) % two_w, 1, stride=1, stride_axis=0)

    for d in range(n_off - 1):
        both = jnp.where(first_half, toeplitz(d, 0), toeplitz(d + 1, GRID_W))
        pair_s[d] = jnp.where(in_window, both, NEG_BIG)

    for r in range(rows):
        rs = min(max(r - NA_WIN_ROWS // 2, 0), rows - NA_WIN_ROWS)
        q = q_ref[r * GRID_W:(r + 1) * GRID_W, :]
        win = slice(rs * GRID_W, (rs + NA_WIN_ROWS) * GRID_W)
        kw = k_ref[win, :].astype(BF16)
        vw = v_ref[win, :].astype(BF16)
        bias = jnp.concatenate([pair_s[rs + 2 * kp - r + NA_WIN_ROWS - 1] for kp in range(NA_WIN_ROWS // 2)],
                               axis=1)
        sw = lax.dot_general(q, kw, NT_DIMS, preferred_element_type=F32) * scale + bias
        sc = lax.dot_general(q, kc, NT_DIMS, preferred_element_type=F32) * scale
        m = jnp.maximum(jnp.max(sw, axis=-1, keepdims=True), jnp.max(sc, axis=-1, keepdims=True))
        pw = jnp.exp(sw - m)
        pc = jnp.exp(sc - m)
        l = jnp.sum(pw, axis=-1, keepdims=True) + jnp.sum(pc, axis=-1, keepdims=True)
        o = (jnp.dot(pw.astype(BF16), vw, preferred_element_type=F32)
             + jnp.dot(pc.astype(BF16), vc, preferred_element_type=F32)) / l
        o_ref[r * GRID_W:(r + 1) * GRID_W, :] = o.astype(o_ref.dtype)


def _na_latent(q, k, v, k_ctx, v_ctx, rpb, n_ctx_rows, n_lat_batch):
    off = n_ctx_rows // DEC_SEQ
    n_off = 2 * NA_WIN_ROWS - 1
    rpb_pad = jnp.zeros((NA_HEADS, 16, 2 * GRID_W), F32).at[:, :n_off, :2 * NA_WIN_COLS - 1].set(rpb)
    q_tok = pl.BlockSpec((DEC_SEQ, NA_HEAD_DIM), lambda b, h: (off + b, h))
    tok = pl.BlockSpec((DEC_SEQ, NA_HEAD_DIM), lambda b, h: (b, h))
    ctx = pl.BlockSpec((None, k_ctx.shape[1], NA_HEAD_DIM), lambda b, h: (b, 0, h))
    return pl.pallas_call(
        _na_lat_kernel,
        grid=(n_lat_batch, NA_HEADS),
        in_specs=[q_tok, tok, tok, ctx, ctx,
                  pl.BlockSpec((None, 16, 2 * GRID_W), lambda b, h: (h, 0, 0))],
        out_specs=pl.BlockSpec((DEC_SEQ, NA_HEAD_DIM), lambda b, h: (b, h)),
        out_shape=jax.ShapeDtypeStruct((n_lat_batch * DEC_SEQ, NA_HEADS * NA_HEAD_DIM), BF16),
        scratch_shapes=[pltpu.VMEM((n_off - 1, GRID_W, 2 * GRID_W), F32)],
        compiler_params=_cp(("parallel", "parallel")),
        name="na_latent",
    )(q, k, v, k_ctx, v_ctx, rpb_pad)


def _ssd_kernel(*refs, seq, has_init):
    n_in = 20 if has_init else 18
    (xs_ref, b_ref, c_ref, z_ref, dt_ref, dtt_ref, cwx_ref, cwb_ref, cwc_ref,
     cbx_ref, cbb_ref, cbc_ref, dtb_ref, dtbt_ref, al_ref, alt_ref, dexp_ref, ng_ref) = refs[:18]
    if has_init:
        if_ref, ib_ref = refs[18:20]
        (y_ref,) = refs[n_in:n_in + 1]
        scratch = refs[n_in + 1:]
    else:
        y_ref, sfo_ref, sbo_ref = refs[n_in:n_in + 3]
        scratch = refs[n_in + 3:]
    pad_s, xs_s, bb_s, cc_s, dt_s, cumc_s, cumr_s, dtr_s, ddt_s, dtot_s, yacc_s, sf_s, sb_s = scratch

    nc = seq // SSD_CHUNK
    ck = SSD_CHUNK
    halo = 8

    def conv_silu(dst_ref, src_ref, w_ref, bias_ref, width):
        pad_s[0:halo, 0:width] = jnp.zeros((halo, width), F32)
        pad_s[seq + halo:seq + 2 * halo, 0:width] = jnp.zeros((halo, width), F32)
        pad_s[halo:seq + halo, 0:width] = src_ref[...]
        for ci in range(nc):
            base = halo - SSD_CONV // 2 + ci * ck
            acc = bias_ref[...] + pad_s[base:base + ck, 0:width] * w_ref[0:1, :]
            for tap in range(1, SSD_CONV):
                acc = acc + pad_s[base + tap:base + tap + ck, 0:width] * w_ref[tap:tap + 1, :]
            dst_ref[ci * ck:(ci + 1) * ck, :] = _silu(acc)

    conv_silu(xs_s, xs_ref, cwx_ref, cbx_ref, GROUP_CH)
    conv_silu(bb_s, b_ref, cwb_ref, cbb_ref, SSD_STATE)
    conv_silu(cc_s, c_ref, cwc_ref, cbc_ref, SSD_STATE)

    li = lax.broadcasted_iota(jnp.int32, (ck, ck), 0)
    si = lax.broadcasted_iota(jnp.int32, (ck, ck), 1)
    lower = li >= si
    upper = si >= li
    tri_lo = jnp.where(lower, 1.0, 0.0).astype(BF16)
    tri_up = jnp.where(upper, 1.0, 0.0).astype(BF16)
    nh2 = 2 * HEADS_PER_GROUP

    dt_s[...] = _softplus(dt_ref[...] + dtb_ref[...])
    a_row = -jnp.exp(al_ref[...])
    dtt = _softplus(dtt_ref[...] + dtbt_ref[...])
    dat = dtt * (-jnp.exp(alt_ref[...]))
    col_is_fwd = lax.broadcasted_iota(jnp.int32, (ck, nh2), 1) < HEADS_PER_GROUP
    row_is_fwd = lax.broadcasted_iota(jnp.int32, (nh2, ck), 0) < HEADS_PER_GROUP
    for c in range(nc):
        da = dt_s[c * ck:(c + 1) * ck, :] * a_row
        cumc_s[c] = jnp.where(col_is_fwd, _dot_01_by_f32(tri_lo, da), _dot_01_by_f32(tri_up, da))
        dat_c = dat[:, c * ck:(c + 1) * ck]
        cumr = jnp.where(row_is_fwd, _dot_f32_by_01(dat_c, tri_up), _dot_f32_by_01(dat_c, tri_lo))
        cumr_s[c] = cumr
        total = jnp.where(row_is_fwd, jnp.broadcast_to(cumr[:, ck - 1:ck], (nh2, ck)),
                          jnp.broadcast_to(cumr[:, 0:1], (nh2, ck)))
        dtt_c = dtt[:, c * ck:(c + 1) * ck]
        dtr_s[c] = dtt_c
        ddt_s[c] = jnp.exp(total - cumr) * dtt_c
        dtot_s[c] = jnp.exp(total)

    lane_lo = lax.broadcasted_iota(jnp.int32, (ck, 2 * SSD_HEAD_DIM), 1) < SSD_HEAD_DIM

    yacc_s[...] = jnp.zeros_like(yacc_s)
    if has_init:
        sf_s[...] = if_ref[...].reshape(GROUP_CH, SSD_STATE).T
        sb_s[...] = ib_ref[...].reshape(GROUP_CH, SSD_STATE).T
    else:
        sf_s[...] = jnp.zeros_like(sf_s)
        sb_s[...] = jnp.zeros_like(sb_s)

    def chunk_step(i, carry):
        for d in (0, 1):
            c = i if d == 0 else nc - 1 - i
            r0 = pl.multiple_of(c * ck, ck)
            x_bf = xs_s[pl.ds(r0, ck), :].astype(BF16)
            b_c = bb_s[pl.ds(r0, ck), :]
            c_f = cc_s[pl.ds(r0, ck), :]
            cumc = cumc_s[c]
            cumr = cumr_s[c]
            dtr = dtr_s[c]
            ddt = ddt_s[c]
            dtot = dtot_s[c]
            state_ref = sf_s if d == 0 else sb_s
            tri_mask = lower if d == 0 else upper

            cb = lax.dot_general(c_f.astype(BF16), b_c.astype(BF16), NT_DIMS, preferred_element_type=F32)
            bt = b_c.T
            state = state_ref[...]
            s_bf = state.astype(BF16)

            y_pieces, st_pieces = [], []
            for hp in range(HEADS_PER_GROUP // 2):
                cols = slice(hp * 2 * SSD_HEAD_DIM, (hp + 1) * 2 * SSD_HEAD_DIM)
                x_pair = x_bf[:, cols]
                rhs = jnp.concatenate([x_pair, s_bf[:, cols]], axis=0)
                ys, sts, dts = [], [], []
                for sub in range(2):
                    j = d * HEADS_PER_GROUP + hp * 2 + sub
                    cum_l = jnp.broadcast_to(cumc[:, j:j + 1], (ck, ck))
                    seg = cum_l - cumr[j:j + 1, :]
                    lmat = jnp.where(tri_mask, jnp.exp(jnp.where(tri_mask, seg, 0.0)), 0.0)
                    lhs = jnp.concatenate([cb * lmat * dtr[j:j + 1, :], c_f * jnp.exp(cum_l)], axis=1)
                    ys.append(jnp.dot(lhs.astype(BF16), rhs, preferred_element_type=F32))
                    w = (bt * ddt[j:j + 1, :]).astype(BF16)
                    sts.append(jnp.dot(w, x_pair, preferred_element_type=F32))
                    dts.append(dtot[j:j + 1, :])
                y_pieces.append(jnp.where(lane_lo, ys[0], ys[1]))
                decay = jnp.where(lane_lo[0:1, :], dts[0], dts[1])
                st_pieces.append(state[:, cols] * decay + jnp.where(lane_lo, sts[0], sts[1]))
            state_ref[...] = jnp.concatenate(st_pieces, axis=1)
            yacc_s[pl.ds(r0, ck), :] = yacc_s[pl.ds(r0, ck), :] + jnp.concatenate(y_pieces, axis=1)
        return carry

    lax.fori_loop(0, nc, chunk_step, 0)

    for ci in range(nc):
        rows = slice(ci * ck, (ci + 1) * ck)
        zz = z_ref[rows, :]
        y = (yacc_s[rows, :] + dexp_ref[...] * xs_s[rows, :]) * _silu(zz)
        y = y * lax.rsqrt(jnp.mean(y * y, axis=-1, keepdims=True) + RMS_EPS) * ng_ref[...]
        y_ref[rows, :] = y.astype(y_ref.dtype)

    if not has_init:
        sfo_ref[...] = sf_s[...].T.reshape(HEADS_PER_GROUP, SSD_HEAD_DIM, SSD_STATE)
        sbo_ref[...] = sb_s[...].T.reshape(HEADS_PER_GROUP, SSD_HEAD_DIM, SSD_STATE)


def _ssd(xbc, z, dtg, dtgt, conv_w8, conv_b, dtb, dtbt, alog, alogt, dexp, norm_g,
         row_start, nb, seq, init=None):
    off = row_start // seq
    g_b = SSD_D_INNER // SSD_STATE
    g_c = g_b + SSD_GROUPS
    has_init = init is not None
    in_specs = [
        pl.BlockSpec((seq, GROUP_CH), lambda b, g: (off + b, g)),
        pl.BlockSpec((seq, SSD_STATE), lambda b, g: (off + b, g_b + g)),
        pl.BlockSpec((seq, SSD_STATE), lambda b, g: (off + b, g_c + g)),
        pl.BlockSpec((seq, GROUP_CH), lambda b, g: (off + b, g)),
        pl.BlockSpec((None, seq, 2 * HEADS_PER_GROUP), lambda b, g: (g, off + b, 0)),
        pl.BlockSpec((None, 2 * HEADS_PER_GROUP, seq), lambda b, g: (g, 0, off + b)),
        pl.BlockSpec((8, GROUP_CH), lambda b, g: (0, g)),
        pl.BlockSpec((8, SSD_STATE), lambda b, g: (0, g_b + g)),
        pl.BlockSpec((8, SSD_STATE), lambda b, g: (0, g_c + g)),
        pl.BlockSpec((1, GROUP_CH), lambda b, g: (0, g)),
        pl.BlockSpec((1, SSD_STATE), lambda b, g: (0, g_b + g)),
        pl.BlockSpec((1, SSD_STATE), lambda b, g: (0, g_c + g)),
        pl.BlockSpec((None, 1, 2 * HEADS_PER_GROUP), lambda b, g: (g, 0, 0)),
        pl.BlockSpec((None, 2 * HEADS_PER_GROUP, 1), lambda b, g: (g, 0, 0)),
        pl.BlockSpec((None, 1, 2 * HEADS_PER_GROUP), lambda b, g: (g, 0, 0)),
        pl.BlockSpec((None, 2 * HEADS_PER_GROUP, 1), lambda b, g: (g, 0, 0)),
        pl.BlockSpec((1, GROUP_CH), lambda b, g: (0, g)),
        pl.BlockSpec((1, GROUP_CH), lambda b, g: (0, g)),
    ]
    args = [xbc, xbc, xbc, z, dtg, dtgt, conv_w8, conv_w8, conv_w8, conv_b, conv_b, conv_b,
            dtb, dtbt, alog, alogt, dexp, norm_g]
    state_spec = pl.BlockSpec((None, HEADS_PER_GROUP, SSD_HEAD_DIM, SSD_STATE), lambda b, g: (b, g, 0, 0))
    y_spec = pl.BlockSpec((seq, GROUP_CH), lambda b, g: (b, g))
    y_shape = jax.ShapeDtypeStruct((nb * seq, SSD_D_INNER), BF16)
    if has_init:
        in_specs += [state_spec, state_spec]
        args += list(init)
        out_specs = y_spec
        out_shape = y_shape
    else:
        st_shape = jax.ShapeDtypeStruct((nb, SSD_HEADS, SSD_HEAD_DIM, SSD_STATE), F32)
        out_specs = [y_spec, state_spec, state_spec]
        out_shape = [y_shape, st_shape, st_shape]
    nc = seq // SSD_CHUNK
    scratch = [
        pltpu.VMEM((seq + 16, GROUP_CH), F32),
        pltpu.VMEM((seq, GROUP_CH), F32),
        pltpu.VMEM((seq, SSD_STATE), F32),
        pltpu.VMEM((seq, SSD_STATE), F32),
        pltpu.VMEM((seq, 2 * HEADS_PER_GROUP), F32),
        pltpu.VMEM((nc, SSD_CHUNK, 2 * HEADS_PER_GROUP), F32),
        pltpu.VMEM((nc, 2 * HEADS_PER_GROUP, SSD_CHUNK), F32),
        pltpu.VMEM((nc, 2 * HEADS_PER_GROUP, SSD_CHUNK), F32),
        pltpu.VMEM((nc, 2 * HEADS_PER_GROUP, SSD_CHUNK), F32),
        pltpu.VMEM((nc, 2 * HEADS_PER_GROUP, SSD_CHUNK), F32),
        pltpu.VMEM((seq, GROUP_CH), F32),
        pltpu.VMEM((SSD_STATE, GROUP_CH), F32),
        pltpu.VMEM((SSD_STATE, GROUP_CH), F32),
    ]
    return pl.pallas_call(
        functools.partial(_ssd_kernel, seq=seq, has_init=has_init),
        grid=(nb, SSD_GROUPS),
        in_specs=in_specs,
        out_specs=out_specs,
        out_shape=out_shape,
        scratch_shapes=scratch,
        compiler_params=_cp(("parallel", "parallel")),
        name="ssd_lat" if has_init else "ssd_ctx",
    )(*args)


def _mix_kernel(nac_ref, nal_ref, ssdc_ref, ssdl_ref, wna_ref, wssd_ref, gna_ref, gssd_ref, o_ref, *, n_ctx_tiles):
    is_ctx = pl.program_id(0) < n_ctx_tiles
    na = jnp.where(is_ctx, nac_ref[...], nal_ref[...])
    ssd = jnp.where(is_ctx, ssdc_ref[...], ssdl_ref[...])
    a = jnp.dot(na, wna_ref[...], preferred_element_type=F32)
    s = jnp.dot(ssd, wssd_ref[...], preferred_element_type=F32)
    o = jax.nn.sigmoid(gna_ref[...]) * a + jax.nn.sigmoid(gssd_ref[...]) * s
    o_ref[...] = o.astype(o_ref.dtype)


def _mix(na_parts, ssd_parts, w_na, w_ssd, gates, n_ctx_rows):
    t = gates.shape[0]
    tm, tn = 512, 512
    nj = D_MODEL // tn
    nct = n_ctx_rows // tm
    ctx_rows = lambda i, j: (_ctx_tile(i, nct), 0)
    lat_rows = lambda i, j: (_lat_tile(i, nct), 0)
    return pl.pallas_call(
        functools.partial(_mix_kernel, n_ctx_tiles=nct),
        grid=(t // tm, nj),
        in_specs=[pl.BlockSpec((tm, w_na.shape[0]), ctx_rows),
                  pl.BlockSpec((tm, w_na.shape[0]), lat_rows),
                  pl.BlockSpec((tm, w_ssd.shape[0]), ctx_rows),
                  pl.BlockSpec((tm, w_ssd.shape[0]), lat_rows),
                  pl.BlockSpec((w_na.shape[0], tn), lambda i, j: (0, j)),
                  pl.BlockSpec((w_ssd.shape[0], tn), lambda i, j: (0, j)),
                  pl.BlockSpec((tm, tn), lambda i, j: (i, j)),
                  pl.BlockSpec((tm, tn), lambda i, j: (i, nj + j))],
        out_specs=pl.BlockSpec((tm, tn), lambda i, j: (i, j)),
        out_shape=jax.ShapeDtypeStruct((t, D_MODEL), BF16),
        compiler_params=_cp(("parallel", "parallel"), vmem_mb=56),
        name="branch_mix",
    )(*na_parts, *ssd_parts, w_na, w_ssd, gates, gates)


def _outproj_kernel(a_ref, w_ref, xc_ref, xl_ref, g_ref, o_ref, *, n_ctx_tiles):
    y = jnp.dot(a_ref[...], w_ref[...], preferred_element_type=F32)
    x = jnp.where(pl.program_id(0) < n_ctx_tiles, xc_ref[...], xl_ref[...])
    o_ref[...] = x + g_ref[...] * y


def _outproj_residual(a, w, x_parts, mod3, gate_chunk, n_ctx_rows):
    t = a.shape[0]
    tm, tn = 1024, 1024
    nj = D_MODEL // tn
    nct = n_ctx_rows // tm
    row = functools.partial(_mod_row, tm=tm, n_ctx_rows=n_ctx_rows)
    return pl.pallas_call(
        functools.partial(_outproj_kernel, n_ctx_tiles=nct),
        grid=(t // tm, nj),
        in_specs=[pl.BlockSpec((tm, a.shape[1]), lambda i, j: (i, 0)),
                  pl.BlockSpec((w.shape[0], tn), lambda i, j: (0, j)),
                  pl.BlockSpec((tm, tn), lambda i, j: (_ctx_tile(i, nct), jnp.where(i < nct, j, nj - 1))),
                  pl.BlockSpec((tm, tn), lambda i, j: (_lat_tile(i, nct), jnp.where(i < nct, 0, j))),
                  pl.BlockSpec((None, 1, tn), lambda i, j: (row(i), 0, gate_chunk * nj + j))],
        out_specs=pl.BlockSpec((tm, tn), lambda i, j: (i, j)),
        out_shape=jax.ShapeDtypeStruct((t, D_MODEL), F32),
        compiler_params=_cp(("parallel", "parallel")),
        name="out_proj",
    )(a, w, *x_parts, mod3)


def _peer_candidate_blocks():
    kk = PEER_TOPK
    blocks = [(a, a + 1, kk // (a + 1)) for a in range(kk // 2)]
    blocks.append((kk // 2, kk, 1))
    return blocks


def _peer_route_kernel(h2t_ref, wqt_ref, keys_ref, s1_ref, e1_ref, s2_ref, e2_ref, qt_s, sv_s, cand_s):
    tt = h2t_ref.shape[1]
    kk = PEER_TOPK
    lanes = 128
    qt_s[...] = jnp.dot(wqt_ref[...], h2t_ref[...], preferred_element_type=F32).astype(BF16)

    def score_step(h, carry):
        for half, s_ref in ((0, s1_ref), (1, s2_ref)):
            c = 2 * h + half
            r0 = pl.multiple_of(c * PEER_HALF, PEER_HALF)
            s_ref[h] = jnp.dot(keys_ref[c], qt_s[pl.ds(r0, PEER_HALF), :], preferred_element_type=F32)
            for lt in range(tt // lanes):
                ln = slice(lt * lanes, (lt + 1) * lanes)
                cur = s_ref[h, :, ln]
                for r in range(kk):
                    m = jnp.max(cur, axis=0, keepdims=True)
                    sv_s[c, r:r + 1, ln] = m
                    cur = jnp.where(cur == m, -jnp.inf, cur)
        return carry

    lax.fori_loop(0, PEER_HEADS, score_step, 0)

    blocks = _peer_candidate_blocks()
    sub = lax.broadcasted_iota(jnp.int32, (8, lanes), 0)

    def head_step(h, carry):
        for lt in range(tt // lanes):
            ln = slice(lt * lanes, (lt + 1) * lanes)
            sv1 = sv_s[2 * h, :, ln]
            sv2 = sv_s[2 * h + 1, :, ln]
            row = 0
            for a_lo, a_hi, n_b in blocks:
                if a_hi - a_lo == 1:
                    n_rows = -(-n_b // 8) * 8
                    blk = sv1[a_lo:a_lo + 1, :] + sv2[0:n_rows, :]
                    if n_b < n_rows:
                        blk = jnp.where(sub < n_b, blk, -jnp.inf)
                else:
                    n_rows = a_hi - a_lo
                    blk = sv1[a_lo:a_hi, :] + sv2[0:1, :]
                cand_s[row:row + n_rows, ln] = blk
                row += n_rows
            cur = cand_s[:, ln]
            m0 = sv1[0:1, :] + sv2[0:1, :]
            taken = jnp.zeros((1, lanes), F32)
            tau = m0
            zsum = jnp.zeros((1, lanes), F32)
            for r in range(kk):
                m = jnp.max(cur, axis=0, keepdims=True)
                eq = cur == m
                cnt = jnp.sum(jnp.where(eq, 1.0, 0.0), axis=0, keepdims=True)
                active = taken < kk
                tau = jnp.where(active, m, tau)
                zsum = zsum + jnp.where(active, cnt * jnp.exp(m - m0), 0.0)
                taken = taken + jnp.where(active, cnt, 0.0)
                cur = jnp.where(eq, -jnp.inf, cur)
            s1 = s1_ref[h, :, ln]
            th1 = jnp.full(s1.shape, jnp.inf, F32)
            for b in range(kk):
                th1 = jnp.where(s1 + sv2[b:b + 1, :] >= tau, sv2[b:b + 1, :], th1)
            s1_ref[h, :, ln] = th1
            e1_ref[h, :, ln] = jnp.exp(s1 - sv1[0:1, :]) / zsum
            e2_ref[h, :, ln] = jnp.exp(s2_ref[h, :, ln] - sv2[0:1, :])
        return carry

    lax.fori_loop(0, PEER_HEADS, head_step, 0)


def _peer_route(h2t, wqt, keys):
    t = h2t.shape[1]
    tt = 512
    n_cand_rows = sum((-(-n_b // 8) * 8) if a_hi - a_lo == 1 else a_hi - a_lo
                      for a_lo, a_hi, n_b in _peer_candidate_blocks())
    slab = pl.BlockSpec((PEER_HEADS, PEER_N_KEYS, tt), lambda i: (0, 0, i))
    slab_shape = jax.ShapeDtypeStruct((PEER_HEADS, PEER_N_KEYS, t), F32)
    return pl.pallas_call(
        _peer_route_kernel,
        grid=(t // tt,),
        in_specs=[pl.BlockSpec((D_MODEL, tt), lambda i: (0, i)),
                  pl.BlockSpec(wqt.shape, lambda i: (0, 0)),
                  pl.BlockSpec(keys.shape, lambda i: (0, 0, 0))],
        out_specs=[slab, slab, slab, slab],
        out_shape=[slab_shape, slab_shape, slab_shape, slab_shape],
        scratch_shapes=[pltpu.VMEM((wqt.shape[0], tt), BF16),
                        pltpu.VMEM((2 * PEER_HEADS, PEER_TOPK, tt), F32),
                        pltpu.VMEM((n_cand_rows, tt), F32)],
        compiler_params=_cp(("parallel",)),
        name="peer_route",
    )(h2t, wqt, keys)


def _gelu_exact(x):
    return 0.5 * x * (1.0 + lax.erf(x * (2.0 ** -0.5)))


PEER_ROWS_PER_TILE = 8
PEER_J_CHUNK = 32
PEER_ROW_GROUP = 8
PEER_ACT_SLAB = 32
PEER_K_SLICE = 256


def _peer_dense_kernel(h2t_ref, u_ref, vt_ref, th1_ref, e1_ref, s2_ref, e2_ref, o_ref, g_s, pre_s, act_s):
    @pl.when(pl.program_id(1) == 0)
    def _():
        o_ref[...] = jnp.zeros_like(o_ref)

    tt = h2t_ref.shape[1]
    jc = PEER_J_CHUNK
    n_k = h2t_ref.shape[0] // PEER_K_SLICE
    chunks_per_k = (tt // 128) * (PEER_N_KEYS // jc) // n_k
    pace = []
    chunk = 0
    for lt in range(tt // 128):
        ln = slice(lt * 128, (lt + 1) * 128)
        for j0 in range(0, PEER_N_KEYS, jc):
            for i0 in range(0, PEER_ROWS_PER_TILE, PEER_ROW_GROUP):
                rows = range(i0, i0 + PEER_ROW_GROUP)
                acc = {ii: None for ii in rows}
                for h in range(PEER_HEADS):
                    s2 = s2_ref[h, j0:j0 + jc, ln]
                    e2 = e2_ref[h, j0:j0 + jc, ln]
                    for ii in rows:
                        term = jnp.where(s2 >= th1_ref[h, ii:ii + 1, ln], e2, 0.0) * e1_ref[h, ii:ii + 1, ln]
                        acc[ii] = term if acc[ii] is None else acc[ii] + term
                for ii in rows:
                    sl = slice(ii * PEER_N_KEYS + j0, ii * PEER_N_KEYS + j0 + jc)
                    g_s[sl, ln] = acc[ii]
            if chunk % chunks_per_k == 0:
                whole = functools.reduce(jnp.add, [acc[ii] for ii in rows])
                pace.append(jnp.minimum(jnp.sum(whole, axis=0, keepdims=True), 0.0))
            chunk += 1

    pre = None
    for k in range(n_k):
        ks = slice(k * PEER_K_SLICE, (k + 1) * PEER_K_SLICE)
        zero = jnp.concatenate([pace[k]] * (tt // 128), axis=1).astype(BF16)
        part = jnp.dot(u_ref[:, ks], h2t_ref[ks, :] + zero, preferred_element_type=F32)
        pre = part if pre is None else pre + part
    pre_s[...] = pre
    for r0 in range(0, g_s.shape[0], PEER_ACT_SLAB):
        sl = slice(r0, r0 + PEER_ACT_SLAB)
        act_s[sl, :] = (g_s[sl, :] * _gelu_exact(pre_s[sl, :])).astype(BF16)
    o_ref[...] += jnp.dot(vt_ref[...], act_s[...], preferred_element_type=F32)


def _peer_dense(h2t, u, vt, th1, e1, s2, e2):
    t = h2t.shape[1]
    n_exp = u.shape[0]
    tt = 512
    te = PEER_ROWS_PER_TILE * PEER_N_KEYS
    rows = pl.BlockSpec((PEER_HEADS, PEER_ROWS_PER_TILE, tt), lambda i, j: (0, j, i))
    slab = pl.BlockSpec((PEER_HEADS, PEER_N_KEYS, tt), lambda i, j: (0, 0, i))
    return pl.pallas_call(
        _peer_dense_kernel,
        grid=(t // tt, n_exp // te),
        in_specs=[pl.BlockSpec((D_MODEL, tt), lambda i, j: (0, i)),
                  pl.BlockSpec((te, D_MODEL), lambda i, j: (j, 0)),
                  pl.BlockSpec((D_MODEL, te), lambda i, j: (0, j)),
                  rows, rows, slab, slab],
        out_specs=pl.BlockSpec((D_MODEL, tt), lambda i, j: (0, i)),
        out_shape=jax.ShapeDtypeStruct((D_MODEL, t), F32),
        scratch_shapes=[pltpu.VMEM((te, tt), F32), pltpu.VMEM((te, tt), F32), pltpu.VMEM((te, tt), BF16)],
        compiler_params=_cp(("parallel", "arbitrary"), vmem_mb=56),
        name="peer_dense",
    )(h2t, u, vt, th1, e1, s2, e2)


def _final_kernel(x_ref, pt_ref, g2_ref, fg_ref, oc_ref, ol_ref, *, n_ctx_tiles):
    x = x_ref[...] + g2_ref[...] * pt_ref[...].T
    y = x * lax.rsqrt(jnp.mean(x * x, axis=-1, keepdims=True) + RMS_EPS) * fg_ref[...]
    ol_ref[...] = y

    @pl.when(pl.program_id(0) < n_ctx_tiles)
    def _():
        oc_ref[...] = y


def _final(x, p_t, mod3, gate_chunk, final_g, n_ctx_rows):
    t = x.shape[0]
    tm = 256
    nct = n_ctx_rows // tm
    row = functools.partial(_mod_row, tm=tm, n_ctx_rows=n_ctx_rows)
    return pl.pallas_call(
        functools.partial(_final_kernel, n_ctx_tiles=nct),
        grid=(t // tm,),
        in_specs=[pl.BlockSpec((tm, D_MODEL), lambda i: (i, 0)),
                  pl.BlockSpec((D_MODEL, tm), lambda i: (0, i)),
                  pl.BlockSpec((None, 1, D_MODEL), lambda i: (row(i), 0, gate_chunk)),
                  pl.BlockSpec((1, D_MODEL), lambda i: (0, 0))],
        out_specs=[pl.BlockSpec((tm, D_MODEL), lambda i: (_ctx_tile(i, nct), 0)),
                   pl.BlockSpec((tm, D_MODEL), lambda i: (_lat_tile(i, nct), 0))],
        out_shape=[jax.ShapeDtypeStruct((n_ctx_rows, D_MODEL), F32),
                   jax.ShapeDtypeStruct((t - n_ctx_rows, D_MODEL), F32)],
        compiler_params=_cp(("arbitrary",)),
        name="final_norm",
    )(x, p_t, mod3, final_g)


def _group_heads(p):
    return jnp.transpose(p.reshape(2, SSD_GROUPS, HEADS_PER_GROUP), (1, 0, 2)).reshape(
        SSD_GROUPS, 2 * HEADS_PER_GROUP)


def _layer(x_parts, lp, mod3, n_ctx_batch, n_lat_batch, cache):
    n_ctx_rows = n_ctx_batch * SEQ
    t = n_ctx_rows + n_lat_batch * DEC_SEQ
    w_in = lp['w_in']
    na_w = NA_HEADS * NA_HEAD_DIM
    o_z = 3 * na_w
    o_xbc = o_z + SSD_D_INNER
    o_dt = o_xbc + SSD_CONV_CH
    o_g = o_dt + 2 * SSD_HEADS

    h1 = _norm_mod(x_parts, lp['norm1_g'][None, :], mod3, 1, 0, n_ctx_rows)
    q = _mm(h1, w_in[:, 0:na_w].astype(BF16), BF16, name="in_q")
    k_new, k_lat = _mm_split(h1, w_in[:, na_w:2 * na_w].astype(BF16), F32, n_ctx_rows, name="in_k")
    v_new, v_lat = _mm_split(h1, w_in[:, 2 * na_w:3 * na_w].astype(BF16), F32, n_ctx_rows, name="in_v")
    z = _mm(h1, w_in[:, o_z:o_xbc].astype(BF16), F32, name="in_z")
    xbc = _mm(h1, w_in[:, o_xbc:o_dt].astype(BF16), F32, name="in_xbc")
    dt_raw = _mm(h1, w_in[:, o_dt:o_g].astype(BF16), F32, name="in_dt")
    gates = _mm(h1, w_in[:, o_g:].astype(BF16), F32, name="in_gates")

    k_ctx, v_ctx, init_f, init_b = cache
    na_ctx = _ctx_attention(q, k_new, v_new, n_ctx_batch)
    na_lat = _na_latent(q, k_lat, v_lat, k_ctx.reshape(n_lat_batch, -1, na_w),
                        v_ctx.reshape(n_lat_batch, -1, na_w), lp['na_rpb'], n_ctx_rows, n_lat_batch)

    dtg = jnp.transpose(dt_raw.reshape(t, 2, SSD_GROUPS, HEADS_PER_GROUP), (2, 0, 1, 3)).reshape(
        SSD_GROUPS, t, 2 * HEADS_PER_GROUP)
    dtgt = jnp.transpose(dtg, (0, 2, 1))
    conv_w8 = jnp.concatenate([lp['conv_w'], jnp.zeros((8 - SSD_CONV, SSD_CONV_CH), F32)], axis=0)
    dtb = _group_heads(lp['ssd_dt_bias'])
    alog = _group_heads(lp['ssd_a_log'])
    dexp = jnp.repeat(lp['ssd_d'], SSD_HEAD_DIM)[None, :]
    ssd_args = (xbc, z, dtg, dtgt, conv_w8, lp['conv_b'][None, :], dtb[:, None, :], dtb[:, :, None],
                alog[:, None, :], alog[:, :, None], dexp, lp['ssd_norm_g'][None, :])
    ssd_ctx, s_f, s_b = _ssd(*ssd_args, row_start=0, nb=n_ctx_batch, seq=SEQ)
    ssd_lat = _ssd(*ssd_args, row_start=n_ctx_rows, nb=n_lat_batch, seq=DEC_SEQ, init=(init_f, init_b))

    mixed = _mix((na_ctx, na_lat), (ssd_ctx, ssd_lat), lp['w_na_proj'].astype(BF16),
                 lp['w_ssd_proj'].astype(BF16), gates, n_ctx_rows)
    x1 = _outproj_residual(mixed, lp['w_out'].astype(BF16), x_parts, mod3, 2, n_ctx_rows)

    h2t = _norm_mod((x1,), lp['norm2_g'][None, :], mod3, 4, 3, n_ctx_rows, transpose_out=True)
    keys = lp['peer_keys'].reshape(2 * PEER_HEADS, PEER_N_KEYS, PEER_HALF).astype(BF16)
    th1, e1, s2, e2 = _peer_route(h2t, lp['peer_wq'].T.astype(BF16), keys)
    peer_t = _peer_dense(h2t, lp['peer_u'].astype(BF16), lp['peer_v'].T.astype(BF16), th1, e1, s2, e2)
    new_k = k_new.reshape(n_ctx_batch, SEQ, NA_HEADS, NA_HEAD_DIM)
    new_v = v_new.reshape(n_ctx_batch, SEQ, NA_HEADS, NA_HEAD_DIM)
    return x1, peer_t, (new_k, new_v, s_f, s_b)


def kernel(x_prompt, x_sample, c, c_ctx, cache_na_k, cache_na_v, state_ssd_fwd, state_ssd_bwd, ada_w, ada_b, norm1_g, norm2_g, w_in, conv_w, conv_b, na_rpb, ssd_a_log, ssd_dt_bias, ssd_d, ssd_norm_g, w_na_proj, w_ssd_proj, w_out, peer_wq, peer_keys, peer_u, peer_v, final_g):
    n_ctx_batch = x_prompt.shape[0]
    n_lat_batch = x_sample.shape[0]
    depth = ada_w.shape[0]
    n_ctx_rows = n_ctx_batch * SEQ
    assert x_prompt.shape[1] == SEQ and x_sample.shape[1] == DEC_SEQ
    assert 1 + n_lat_batch <= MOD_ROWS and n_ctx_rows % DEC_SEQ == 0
    assert depth == 1, "the final-norm kernel folds in the PEER residual of the single trunk layer"

    x_parts = (x_prompt.reshape(-1, D_MODEL), x_sample.reshape(-1, D_MODEL))
    cvec = jnp.concatenate([c_ctx[None, :], c, jnp.zeros((MOD_ROWS - 1 - n_lat_batch, D_MODEL), F32)], axis=0)
    lp = {'norm1_g': norm1_g[0], 'norm2_g': norm2_g[0], 'w_in': w_in[0], 'conv_w': conv_w[0],
          'conv_b': conv_b[0], 'na_rpb': na_rpb[0], 'ssd_a_log': ssd_a_log[0],
          'ssd_dt_bias': ssd_dt_bias[0], 'ssd_d': ssd_d[0], 'ssd_norm_g': ssd_norm_g[0],
          'w_na_proj': w_na_proj[0], 'w_ssd_proj': w_ssd_proj[0], 'w_out': w_out[0],
          'peer_wq': peer_wq[0], 'peer_keys': peer_keys[0], 'peer_u': peer_u[0], 'peer_v': peer_v[0]}
    mod3 = _ada(cvec, ada_w[0], ada_b[0][None, :])[:, None, :]
    cache = (cache_na_k[:, 0], cache_na_v[:, 0], state_ssd_fwd[:, 0], state_ssd_bwd[:, 0])
    x1, peer_t, (new_k, new_v, new_f, new_b) = _layer(x_parts, lp, mod3, n_ctx_batch, n_lat_batch, cache)
    y_ctx, y_lat = _final(x1, peer_t, mod3, 5, final_g[None, :], n_ctx_rows)
    return (y_ctx.reshape(x_prompt.shape), y_lat.reshape(x_sample.shape),
            new_k[:, None], new_v[:, None], new_f[:, None], new_b[:, None])
```

```python
import functools

import jax
import jax.numpy as jnp
from jax import lax
from jax.experimental import pallas as pl
from jax.experimental.pallas import tpu as pltpu

F32 = jnp.float32
BF16 = jnp.bfloat16

D_MODEL = 2048
SEQ = 256
DEC_SEQ = 1024
GRID_W = 64
NA_HEADS = 16
NA_HEAD_DIM = 128
NA_WIN_ROWS = 8
NA_WIN_COLS = 16
SSD_D_INNER = 4096
SSD_HEAD_DIM = 64
SSD_HEADS = 64
SSD_GROUPS = 8
SSD_STATE = 128
SSD_CONV = 5
SSD_CHUNK = 128
SSD_CONV_CH = SSD_D_INNER + 2 * SSD_GROUPS * SSD_STATE
HEADS_PER_GROUP = SSD_HEADS // SSD_GROUPS
GROUP_CH = SSD_D_INNER // SSD_GROUPS
PEER_HEADS = 8
PEER_HALF = 128
PEER_N_KEYS = 128
PEER_TOPK = 16
RMS_EPS = 1e-6
NEG_BIG = -1e30
MOD_ROWS = 8
VMEM_LIMIT_MB = 48

NT_DIMS = (((1,), (1,)), ((), ()))


def _cp(sem, vmem_mb=VMEM_LIMIT_MB, flags=None):
    return pltpu.CompilerParams(dimension_semantics=sem, vmem_limit_bytes=vmem_mb * 1024 * 1024, flags=flags)


def _mod_row(i, tm, n_ctx_rows):
    n_ctx_tiles = n_ctx_rows // tm
    return jnp.where(i < n_ctx_tiles, 0, 1 + (i - n_ctx_tiles) // (DEC_SEQ // tm))


def _silu(x):
    return x * jax.nn.sigmoid(x)


def _softplus(x):
    return jnp.maximum(x, 0.0) + jnp.log1p(jnp.exp(-jnp.abs(x)))


def _split3(x):
    hi = x.astype(BF16)
    r1 = x - hi.astype(F32)
    mid = r1.astype(BF16)
    lo = (r1 - mid.astype(F32)).astype(BF16)
    return hi, mid, lo


def _dot_f32_by_01(x, m01):
    return sum(jnp.dot(p, m01, preferred_element_type=F32) for p in _split3(x))


def _dot_01_by_f32(m01, x):
    return sum(jnp.dot(m01, p, preferred_element_type=F32) for p in _split3(x))


def _ada_kernel(c_ref, w_ref, b_ref, o_ref):
    s = _silu(c_ref[...])
    o_ref[...] = jnp.dot(s, w_ref[...], preferred_element_type=F32,
                         precision=lax.Precision.HIGHEST) + b_ref[...]


def _ada(cvec, ada_w, ada_b):
    n = ada_w.shape[1]
    tn = 1024
    return pl.pallas_call(
        _ada_kernel,
        grid=(n // tn,),
        in_specs=[pl.BlockSpec((MOD_ROWS, D_MODEL), lambda j: (0, 0)),
                  pl.BlockSpec((D_MODEL, tn), lambda j: (0, j)),
                  pl.BlockSpec((1, tn), lambda j: (0, j))],
        out_specs=pl.BlockSpec((MOD_ROWS, tn), lambda j: (0, j)),
        out_shape=jax.ShapeDtypeStruct((MOD_ROWS, n), F32),
        compiler_params=_cp(("parallel",)),
        name="ada_mod",
    )(cvec, ada_w, ada_b)


def _norm_mod_kernel(*refs, transpose_out, n_ctx_tiles):
    if n_ctx_tiles is None:
        x_ref, g_ref, sc_ref, sh_ref, o_ref = refs
        x = x_ref[...]
    else:
        xc_ref, xl_ref, g_ref, sc_ref, sh_ref, o_ref = refs
        x = jnp.where(pl.program_id(0) < n_ctx_tiles, xc_ref[...], xl_ref[...])
    y = x * lax.rsqrt(jnp.mean(x * x, axis=-1, keepdims=True) + RMS_EPS) * g_ref[...]
    y = y * (1.0 + sc_ref[...]) + sh_ref[...]
    o_ref[...] = (y.T if transpose_out else y).astype(o_ref.dtype)


def _norm_mod(x_parts, g, mod3, sc_chunk, sh_chunk, n_ctx_rows, transpose_out=False):
    t = sum(p.shape[0] for p in x_parts)
    tm = 256
    nct = n_ctx_rows // tm
    row = functools.partial(_mod_row, tm=tm, n_ctx_rows=n_ctx_rows)
    if len(x_parts) == 1:
        x_specs = [pl.BlockSpec((tm, D_MODEL), lambda i: (i, 0))]
    else:
        x_specs = [pl.BlockSpec((tm, D_MODEL), lambda i: (_ctx_tile(i, nct), 0)),
                   pl.BlockSpec((tm, D_MODEL), lambda i: (_lat_tile(i, nct), 0))]
    if transpose_out:
        out_spec = pl.BlockSpec((D_MODEL, tm), lambda i: (0, i))
        out_shape = jax.ShapeDtypeStruct((D_MODEL, t), BF16)
    else:
        out_spec = pl.BlockSpec((tm, D_MODEL), lambda i: (i, 0))
        out_shape = jax.ShapeDtypeStruct((t, D_MODEL), BF16)
    return pl.pallas_call(
        functools.partial(_norm_mod_kernel, transpose_out=transpose_out,
                          n_ctx_tiles=None if len(x_parts) == 1 else nct),
        grid=(t // tm,),
        in_specs=x_specs + [pl.BlockSpec((1, D_MODEL), lambda i: (0, 0)),
                            pl.BlockSpec((None, 1, D_MODEL), lambda i: (row(i), 0, sc_chunk)),
                            pl.BlockSpec((None, 1, D_MODEL), lambda i: (row(i), 0, sh_chunk))],
        out_specs=out_spec,
        out_shape=out_shape,
        compiler_params=_cp(("parallel",)),
        name="norm_mod",
    )(*x_parts, g, mod3, mod3)


def _mm_kernel(a_ref, w_ref, o_ref, wb_s):
    @pl.when(pl.program_id(1) == 0)
    def _():
        wb_s[...] = w_ref[...].astype(BF16)

    o_ref[...] = jnp.dot(a_ref[...], wb_s[...], preferred_element_type=F32).astype(o_ref.dtype)


def _mm(a, w, col0, n, out_dtype, tm=1024, tn=1024, name="mm"):
    m, k = a.shape
    tn = min(tn, n)
    assert col0 % tn == 0 and n % tn == 0 and m % tm == 0
    c0 = col0 // tn
    return pl.pallas_call(
        _mm_kernel,
        grid=(n // tn, m // tm),
        in_specs=[pl.BlockSpec((tm, k), lambda j, i: (i, 0)),
                  pl.BlockSpec((k, tn), lambda j, i: (0, c0 + j))],
        out_specs=pl.BlockSpec((tm, tn), lambda j, i: (i, j)),
        out_shape=jax.ShapeDtypeStruct((m, n), out_dtype),
        scratch_shapes=[pltpu.VMEM((k, tn), BF16)],
        compiler_params=_cp(("arbitrary", "arbitrary"), vmem_mb=56),
        name=name,
    )(a, w)


def _ctx_tile(i, n_ctx_tiles):
    return jnp.minimum(i, n_ctx_tiles - 1)


def _lat_tile(i, n_ctx_tiles):
    return jnp.maximum(i - n_ctx_tiles, 0)


def _mm_split_kernel(a_ref, w_ref, oc_ref, ol_ref, wb_s, *, n_ctx_tiles):
    @pl.when(pl.program_id(1) == 0)
    def _():
        wb_s[...] = w_ref[...].astype(BF16)

    y = jnp.dot(a_ref[...], wb_s[...], preferred_element_type=F32)
    ol_ref[...] = y.astype(ol_ref.dtype)

    @pl.when(pl.program_id(1) < n_ctx_tiles)
    def _():
        oc_ref[...] = y.astype(oc_ref.dtype)


def _mm_split(a, w, col0, n, out_dtype, n_ctx_rows, tm=1024, tn=1024, name="mm_split"):
    m, k = a.shape
    assert col0 % tn == 0 and n % tn == 0 and m % tm == 0 and n_ctx_rows % tm == 0
    nct = n_ctx_rows // tm
    c0 = col0 // tn
    return pl.pallas_call(
        functools.partial(_mm_split_kernel, n_ctx_tiles=nct),
        grid=(n // tn, m // tm),
        in_specs=[pl.BlockSpec((tm, k), lambda j, i: (i, 0)),
                  pl.BlockSpec((k, tn), lambda j, i: (0, c0 + j))],
        out_specs=[pl.BlockSpec((tm, tn), lambda j, i: (_ctx_tile(i, nct), j)),
                   pl.BlockSpec((tm, tn), lambda j, i: (_lat_tile(i, nct), j))],
        out_shape=[jax.ShapeDtypeStruct((n_ctx_rows, n), out_dtype),
                   jax.ShapeDtypeStruct((m - n_ctx_rows, n), out_dtype)],
        scratch_shapes=[pltpu.VMEM((k, tn), BF16)],
        compiler_params=_cp(("arbitrary", "arbitrary"), vmem_mb=56),
        name=name,
    )(a, w)


def _ctx_attn_kernel(q_ref, k_ref, v_ref, o_ref):
    scale = NA_HEAD_DIM ** -0.5
    for h in range(NA_HEADS):
        sl = slice(h * NA_HEAD_DIM, (h + 1) * NA_HEAD_DIM)
        q = q_ref[:, sl]
        k = k_ref[:, sl].astype(BF16)
        v = v_ref[:, sl].astype(BF16)
        s = lax.dot_general(q, k, NT_DIMS, preferred_element_type=F32) * scale
        p = jnp.exp(s - jnp.max(s, axis=-1, keepdims=True))
        l = jnp.sum(p, axis=-1, keepdims=True)
        o = jnp.dot(p.astype(BF16), v, preferred_element_type=F32) / l
        o_ref[:, sl] = o.astype(o_ref.dtype)


def _ctx_attention(q, k, v, n_ctx_batch):
    width = NA_HEADS * NA_HEAD_DIM
    spec = pl.BlockSpec((SEQ, width), lambda b: (b, 0))
    return pl.pallas_call(
        _ctx_attn_kernel,
        grid=(n_ctx_batch,),
        in_specs=[spec, spec, spec],
        out_specs=spec,
        out_shape=jax.ShapeDtypeStruct((n_ctx_batch * SEQ, width), BF16),
        compiler_params=_cp(("parallel",)),
        name="ctx_attn",
    )(q, k, v)


def _na_lat_kernel(q_ref, k_ref, v_ref, kc_ref, vc_ref, rpb_ref, o_ref, pair_s):
    scale = NA_HEAD_DIM ** -0.5
    rows = DEC_SEQ // GRID_W
    kc = kc_ref[...].astype(BF16)
    vc = vc_ref[...].astype(BF16)

    two_w = 2 * GRID_W
    qi = lax.broadcasted_iota(jnp.int32, (GRID_W, two_w), 0)
    lane = lax.broadcasted_iota(jnp.int32, (GRID_W, two_w), 1)
    kj = lane & (GRID_W - 1)
    col_start = jnp.clip(qi - NA_WIN_COLS // 2, 0, GRID_W - NA_WIN_COLS)
    in_window = (kj >= col_start) & (kj < col_start + NA_WIN_COLS)
    first_half = lane < GRID_W
    n_off = 2 * NA_WIN_ROWS - 1

    def toeplitz(d, lane_off):
        row = jnp.broadcast_to(rpb_ref[d:d + 1, :], (GRID_W, two_w))
        return pltpu.roll(row, (lane_off - (NA_WIN_COLS - 1)) % two_w, 1, stride=1, stride_axis=0)

    for d in range(n_off - 1):
        both = jnp.where(first_half, toeplitz(d, 0), toeplitz(d + 1, GRID_W))
        pair_s[d] = jnp.where(in_window, both, NEG_BIG)

    for r in range(rows):
        rs = min(max(r - NA_WIN_ROWS // 2, 0), rows - NA_WIN_ROWS)
        q = q_ref[r * GRID_W:(r + 1) * GRID_W, :]
        win = slice(rs * GRID_W, (rs + NA_WIN_ROWS) * GRID_W)
        kw = k_ref[win, :].astype(BF16)
        vw = v_ref[win, :].astype(BF16)
        bias = jnp.concatenate([pair_s[rs + 2 * kp - r + NA_WIN_ROWS - 1] for kp in range(NA_WIN_ROWS // 2)],
                               axis=1)
        sw = lax.dot_general(q, kw, NT_DIMS, preferred_element_type=F32) * scale + bias
        sc = lax.dot_general(q, kc, NT_DIMS, preferred_element_type=F32) * scale
        m = jnp.maximum(jnp.max(sw, axis=-1, keepdims=True), jnp.max(sc, axis=-1, keepdims=True))
        pw = jnp.exp(sw - m)
        pc = jnp.exp(sc - m)
        l = jnp.sum(pw, axis=-1, keepdims=True) + jnp.sum(pc, axis=-1, keepdims=True)
        o = (jnp.dot(pw.astype(BF16), vw, preferred_element_type=F32)
             + jnp.dot(pc.astype(BF16), vc, preferred_element_type=F32)) / l
        o_ref[r * GRID_W:(r + 1) * GRID_W, :] = o.astype(o_ref.dtype)


def _na_latent(q, k, v, k_ctx, v_ctx, rpb, n_ctx_rows, n_lat_batch):
    off = n_ctx_rows // DEC_SEQ
    n_off = 2 * NA_WIN_ROWS - 1
    rpb_pad = jnp.zeros((NA_HEADS, 16, 2 * GRID_W), F32).at[:, :n_off, :2 * NA_WIN_COLS - 1].set(rpb)
    q_tok = pl.BlockSpec((DEC_SEQ, NA_HEAD_DIM), lambda b, h: (off + b, h))
    tok = pl.BlockSpec((DEC_SEQ, NA_HEAD_DIM), lambda b, h: (b, h))
    ctx = pl.BlockSpec((None, k_ctx.shape[1], NA_HEAD_DIM), lambda b, h: (b, 0, h))
    return pl.pallas_call(
        _na_lat_kernel,
        grid=(n_lat_batch, NA_HEADS),
        in_specs=[q_tok, tok, tok, ctx, ctx,
                  pl.BlockSpec((None, 16, 2 * GRID_W), lambda b, h: (h, 0, 0))],
        out_specs=pl.BlockSpec((DEC_SEQ, NA_HEAD_DIM), lambda b, h: (b, h)),
        out_shape=jax.ShapeDtypeStruct((n_lat_batch * DEC_SEQ, NA_HEADS * NA_HEAD_DIM), BF16),
        scratch_shapes=[pltpu.VMEM((n_off - 1, GRID_W, 2 * GRID_W), F32)],
        compiler_params=_cp(("parallel", "parallel")),
        name="na_latent",
    )(q, k, v, k_ctx, v_ctx, rpb_pad)


def _ssd_kernel(*refs, seq, has_init):
    n_in = 20 if has_init else 18
    (xs_ref, b_ref, c_ref, z_ref, dt_ref, dtt_ref, cwx_ref, cwb_ref, cwc_ref,
     cbx_ref, cbb_ref, cbc_ref, dtb_ref, dtbt_ref, al_ref, alt_ref, dexp_ref, ng_ref) = refs[:18]
    if has_init:
        if_ref, ib_ref = refs[18:20]
        (y_ref,) = refs[n_in:n_in + 1]
        scratch = refs[n_in + 1:]
    else:
        y_ref, sfo_ref, sbo_ref = refs[n_in:n_in + 3]
        scratch = refs[n_in + 3:]
    pad_s, xs_s, bb_s, cc_s, dt_s, cumc_s, cumr_s, dtr_s, ddt_s, dtot_s, yacc_s, sf_s, sb_s = scratch

    nc = seq // SSD_CHUNK
    ck = SSD_CHUNK
    halo = 8

    def conv_silu(dst_ref, src_ref, w_ref, bias_ref, width):
        pad_s[0:halo, 0:width] = jnp.zeros((halo, width), F32)
        pad_s[seq + halo:seq + 2 * halo, 0:width] = jnp.zeros((halo, width), F32)
        pad_s[halo:seq + halo, 0:width] = src_ref[...]
        for ci in range(nc):
            base = halo - SSD_CONV // 2 + ci * ck
            acc = bias_ref[...] + pad_s[base:base + ck, 0:width] * w_ref[0:1, :]
            for tap in range(1, SSD_CONV):
                acc = acc + pad_s[base + tap:base + tap + ck, 0:width] * w_ref[tap:tap + 1, :]
            dst_ref[ci * ck:(ci + 1) * ck, :] = _silu(acc)

    conv_silu(xs_s, xs_ref, cwx_ref, cbx_ref, GROUP_CH)
    conv_silu(bb_s, b_ref, cwb_ref, cbb_ref, SSD_STATE)
    conv_silu(cc_s, c_ref, cwc_ref, cbc_ref, SSD_STATE)

    li = lax.broadcasted_iota(jnp.int32, (ck, ck), 0)
    si = lax.broadcasted_iota(jnp.int32, (ck, ck), 1)
    lower = li >= si
    upper = si >= li
    tri_lo = jnp.where(lower, 1.0, 0.0).astype(BF16)
    tri_up = jnp.where(upper, 1.0, 0.0).astype(BF16)
    nh2 = 2 * HEADS_PER_GROUP

    dt_s[...] = _softplus(dt_ref[...] + dtb_ref[...])
    a_row = -jnp.exp(al_ref[...])
    dtt = _softplus(dtt_ref[...] + dtbt_ref[...])
    dat = dtt * (-jnp.exp(alt_ref[...]))
    col_is_fwd = lax.broadcasted_iota(jnp.int32, (ck, nh2), 1) < HEADS_PER_GROUP
    row_is_fwd = lax.broadcasted_iota(jnp.int32, (nh2, ck), 0) < HEADS_PER_GROUP
    for c in range(nc):
        da = dt_s[c * ck:(c + 1) * ck, :] * a_row
        cumc_s[c] = jnp.where(col_is_fwd, _dot_01_by_f32(tri_lo, da), _dot_01_by_f32(tri_up, da))
        dat_c = dat[:, c * ck:(c + 1) * ck]
        cumr = jnp.where(row_is_fwd, _dot_f32_by_01(dat_c, tri_up), _dot_f32_by_01(dat_c, tri_lo))
        cumr_s[c] = cumr
        total = jnp.where(row_is_fwd, jnp.broadcast_to(cumr[:, ck - 1:ck], (nh2, ck)),
                          jnp.broadcast_to(cumr[:, 0:1], (nh2, ck)))
        dtt_c = dtt[:, c * ck:(c + 1) * ck]
        dtr_s[c] = dtt_c
        ddt_s[c] = jnp.exp(total - cumr) * dtt_c
        dtot_s[c] = jnp.exp(total)

    lane_lo = lax.broadcasted_iota(jnp.int32, (ck, 2 * SSD_HEAD_DIM), 1) < SSD_HEAD_DIM

    yacc_s[...] = jnp.zeros_like(yacc_s)
    if has_init:
        sf_s[...] = if_ref[...].reshape(GROUP_CH, SSD_STATE).T
        sb_s[...] = ib_ref[...].reshape(GROUP_CH, SSD_STATE).T
    else:
        sf_s[...] = jnp.zeros_like(sf_s)
        sb_s[...] = jnp.zeros_like(sb_s)

    def chunk_step(i, carry):
        for d in (0, 1):
            c = i if d == 0 else nc - 1 - i
            r0 = pl.multiple_of(c * ck, ck)
            x_bf = xs_s[pl.ds(r0, ck), :].astype(BF16)
            b_c = bb_s[pl.ds(r0, ck), :]
            c_f = cc_s[pl.ds(r0, ck), :]
            cumc = cumc_s[c]
            cumr = cumr_s[c]
            dtr = dtr_s[c]
            ddt = ddt_s[c]
            dtot = dtot_s[c]
            state_ref = sf_s if d == 0 else sb_s
            tri_mask = lower if d == 0 else upper

            cb = lax.dot_general(c_f.astype(BF16), b_c.astype(BF16), NT_DIMS, preferred_element_type=F32)
            bt = b_c.T
            state = state_ref[...]
            s_bf = state.astype(BF16)

            y_pieces, st_pieces = [], []
            for hp in range(HEADS_PER_GROUP // 2):
                cols = slice(hp * 2 * SSD_HEAD_DIM, (hp + 1) * 2 * SSD_HEAD_DIM)
                x_pair = x_bf[:, cols]
                rhs = jnp.concatenate([x_pair, s_bf[:, cols]], axis=0)
                ys, sts, dts = [], [], []
                for sub in range(2):
                    j = d * HEADS_PER_GROUP + hp * 2 + sub
                    cum_l = jnp.broadcast_to(cumc[:, j:j + 1], (ck, ck))
                    seg = cum_l - cumr[j:j + 1, :]
                    lmat = jnp.where(tri_mask, jnp.exp(seg), 0.0)
                    lhs = jnp.concatenate([cb * lmat * dtr[j:j + 1, :], c_f * jnp.exp(cum_l)], axis=1)
                    ys.append(jnp.dot(lhs.astype(BF16), rhs, preferred_element_type=F32))
                    w = (bt * ddt[j:j + 1, :]).astype(BF16)
                    sts.append(jnp.dot(w, x_pair, preferred_element_type=F32))
                    dts.append(dtot[j:j + 1, :])
                y_pieces.append(jnp.where(lane_lo, ys[0], ys[1]))
                decay = jnp.where(lane_lo[0:1, :], dts[0], dts[1])
                st_pieces.append(state[:, cols] * decay + jnp.where(lane_lo, sts[0], sts[1]))
            state_ref[...] = jnp.concatenate(st_pieces, axis=1)
            yacc_s[pl.ds(r0, ck), :] = yacc_s[pl.ds(r0, ck), :] + jnp.concatenate(y_pieces, axis=1)
        return carry

    lax.fori_loop(0, nc, chunk_step, 0)

    for ci in range(nc):
        rows = slice(ci * ck, (ci + 1) * ck)
        zz = z_ref[rows, :]
        y = (yacc_s[rows, :] + dexp_ref[...] * xs_s[rows, :]) * _silu(zz)
        y = y * lax.rsqrt(jnp.mean(y * y, axis=-1, keepdims=True) + RMS_EPS) * ng_ref[...]
        y_ref[rows, :] = y.astype(y_ref.dtype)

    if not has_init:
        sfo_ref[...] = sf_s[...].T.reshape(HEADS_PER_GROUP, SSD_HEAD_DIM, SSD_STATE)
        sbo_ref[...] = sb_s[...].T.reshape(HEADS_PER_GROUP, SSD_HEAD_DIM, SSD_STATE)


def _ssd(xbc, z, dtg, dtgt, conv_w8, conv_b, dtb, dtbt, alog, alogt, dexp, norm_g,
         row_start, nb, seq, init=None):
    off = row_start // seq
    g_b = SSD_D_INNER // SSD_STATE
    g_c = g_b + SSD_GROUPS
    has_init = init is not None
    in_specs = [
        pl.BlockSpec((seq, GROUP_CH), lambda b, g: (off + b, g)),
        pl.BlockSpec((seq, SSD_STATE), lambda b, g: (off + b, g_b + g)),
        pl.BlockSpec((seq, SSD_STATE), lambda b, g: (off + b, g_c + g)),
        pl.BlockSpec((seq, GROUP_CH), lambda b, g: (off + b, g)),
        pl.BlockSpec((None, seq, 2 * HEADS_PER_GROUP), lambda b, g: (g, off + b, 0)),
        pl.BlockSpec((None, 2 * HEADS_PER_GROUP, seq), lambda b, g: (g, 0, off + b)),
        pl.BlockSpec((8, GROUP_CH), lambda b, g: (0, g)),
        pl.BlockSpec((8, SSD_STATE), lambda b, g: (0, g_b + g)),
        pl.BlockSpec((8, SSD_STATE), lambda b, g: (0, g_c + g)),
        pl.BlockSpec((1, GROUP_CH), lambda b, g: (0, g)),
        pl.BlockSpec((1, SSD_STATE), lambda b, g: (0, g_b + g)),
        pl.BlockSpec((1, SSD_STATE), lambda b, g: (0, g_c + g)),
        pl.BlockSpec((None, 1, 2 * HEADS_PER_GROUP), lambda b, g: (g, 0, 0)),
        pl.BlockSpec((None, 2 * HEADS_PER_GROUP, 1), lambda b, g: (g, 0, 0)),
        pl.BlockSpec((None, 1, 2 * HEADS_PER_GROUP), lambda b, g: (g, 0, 0)),
        pl.BlockSpec((None, 2 * HEADS_PER_GROUP, 1), lambda b, g: (g, 0, 0)),
        pl.BlockSpec((1, GROUP_CH), lambda b, g: (0, g)),
        pl.BlockSpec((1, GROUP_CH), lambda b, g: (0, g)),
    ]
    args = [xbc, xbc, xbc, z, dtg, dtgt, conv_w8, conv_w8, conv_w8, conv_b, conv_b, conv_b,
            dtb, dtbt, alog, alogt, dexp, norm_g]
    state_spec = pl.BlockSpec((None, HEADS_PER_GROUP, SSD_HEAD_DIM, SSD_STATE), lambda b, g: (b, g, 0, 0))
    y_spec = pl.BlockSpec((seq, GROUP_CH), lambda b, g: (b, g))
    y_shape = jax.ShapeDtypeStruct((nb * seq, SSD_D_INNER), BF16)
    if has_init:
        in_specs += [state_spec, state_spec]
        args += list(init)
        out_specs = y_spec
        out_shape = y_shape
    else:
        st_shape = jax.ShapeDtypeStruct((nb, SSD_HEADS, SSD_HEAD_DIM, SSD_STATE), F32)
        out_specs = [y_spec, state_spec, state_spec]
        out_shape = [y_shape, st_shape, st_shape]
    nc = seq // SSD_CHUNK
    scratch = [
        pltpu.VMEM((seq + 16, GROUP_CH), F32),
        pltpu.VMEM((seq, GROUP_CH), F32),
        pltpu.VMEM((seq, SSD_STATE), F32),
        pltpu.VMEM((seq, SSD_STATE), F32),
        pltpu.VMEM((seq, 2 * HEADS_PER_GROUP), F32),
        pltpu.VMEM((nc, SSD_CHUNK, 2 * HEADS_PER_GROUP), F32),
        pltpu.VMEM((nc, 2 * HEADS_PER_GROUP, SSD_CHUNK), F32),
        pltpu.VMEM((nc, 2 * HEADS_PER_GROUP, SSD_CHUNK), F32),
        pltpu.VMEM((nc, 2 * HEADS_PER_GROUP, SSD_CHUNK), F32),
        pltpu.VMEM((nc, 2 * HEADS_PER_GROUP, SSD_CHUNK), F32),
        pltpu.VMEM((seq, GROUP_CH), F32),
        pltpu.VMEM((SSD_STATE, GROUP_CH), F32),
        pltpu.VMEM((SSD_STATE, GROUP_CH), F32),
    ]
    return pl.pallas_call(
        functools.partial(_ssd_kernel, seq=seq, has_init=has_init),
        grid=(nb, SSD_GROUPS),
        in_specs=in_specs,
        out_specs=out_specs,
        out_shape=out_shape,
        scratch_shapes=scratch,
        compiler_params=_cp(("parallel", "parallel")),
        name="ssd_lat" if has_init else "ssd_ctx",
    )(*args)


def _mix_kernel(nac_ref, nal_ref, ssdc_ref, ssdl_ref, wna_ref, wssd_ref, gna_ref, gssd_ref, o_ref, *, n_ctx_tiles):
    is_ctx = pl.program_id(0) < n_ctx_tiles
    na = jnp.where(is_ctx, nac_ref[...], nal_ref[...])
    ssd = jnp.where(is_ctx, ssdc_ref[...], ssdl_ref[...])
    a = jnp.dot(na, wna_ref[...], preferred_element_type=F32)
    s = jnp.dot(ssd, wssd_ref[...], preferred_element_type=F32)
    o = jax.nn.sigmoid(gna_ref[...]) * a + jax.nn.sigmoid(gssd_ref[...]) * s
    o_ref[...] = o.astype(o_ref.dtype)


def _mix(na_parts, ssd_parts, w_na, w_ssd, gates, n_ctx_rows):
    t = gates.shape[0]
    tm, tn = 512, 512
    nj = D_MODEL // tn
    nct = n_ctx_rows // tm
    ctx_rows = lambda i, j: (_ctx_tile(i, nct), 0)
    lat_rows = lambda i, j: (_lat_tile(i, nct), 0)
    return pl.pallas_call(
        functools.partial(_mix_kernel, n_ctx_tiles=nct),
        grid=(t // tm, nj),
        in_specs=[pl.BlockSpec((tm, w_na.shape[0]), ctx_rows),
                  pl.BlockSpec((tm, w_na.shape[0]), lat_rows),
                  pl.BlockSpec((tm, w_ssd.shape[0]), ctx_rows),
                  pl.BlockSpec((tm, w_ssd.shape[0]), lat_rows),
                  pl.BlockSpec((w_na.shape[0], tn), lambda i, j: (0, j)),
                  pl.BlockSpec((w_ssd.shape[0], tn), lambda i, j: (0, j)),
                  pl.BlockSpec((tm, tn), lambda i, j: (i, j)),
                  pl.BlockSpec((tm, tn), lambda i, j: (i, nj + j))],
        out_specs=pl.BlockSpec((tm, tn), lambda i, j: (i, j)),
        out_shape=jax.ShapeDtypeStruct((t, D_MODEL), BF16),
        compiler_params=_cp(("parallel", "parallel"), vmem_mb=56),
        name="branch_mix",
    )(*na_parts, *ssd_parts, w_na, w_ssd, gates, gates)


def _outproj_kernel(a_ref, w_ref, xc_ref, xl_ref, g_ref, o_ref, *, n_ctx_tiles):
    y = jnp.dot(a_ref[...], w_ref[...], preferred_element_type=F32)
    x = jnp.where(pl.program_id(0) < n_ctx_tiles, xc_ref[...], xl_ref[...])
    o_ref[...] = x + g_ref[...] * y


def _outproj_residual(a, w, x_parts, mod3, gate_chunk, n_ctx_rows):
    t = a.shape[0]
    tm, tn = 1024, 1024
    nj = D_MODEL // tn
    nct = n_ctx_rows // tm
    row = functools.partial(_mod_row, tm=tm, n_ctx_rows=n_ctx_rows)
    return pl.pallas_call(
        functools.partial(_outproj_kernel, n_ctx_tiles=nct),
        grid=(t // tm, nj),
        in_specs=[pl.BlockSpec((tm, a.shape[1]), lambda i, j: (i, 0)),
                  pl.BlockSpec((w.shape[0], tn), lambda i, j: (0, j)),
                  pl.BlockSpec((tm, tn), lambda i, j: (_ctx_tile(i, nct), jnp.where(i < nct, j, nj - 1))),
                  pl.BlockSpec((tm, tn), lambda i, j: (_lat_tile(i, nct), jnp.where(i < nct, 0, j))),
                  pl.BlockSpec((None, 1, tn), lambda i, j: (row(i), 0, gate_chunk * nj + j))],
        out_specs=pl.BlockSpec((tm, tn), lambda i, j: (i, j)),
        out_shape=jax.ShapeDtypeStruct((t, D_MODEL), F32),
        compiler_params=_cp(("parallel", "parallel")),
        name="out_proj",
    )(a, w, *x_parts, mod3)


def _peer_candidate_blocks():
    kk = PEER_TOPK
    blocks = [(a, a + 1, kk // (a + 1)) for a in range(kk // 2)]
    blocks.append((kk // 2, kk, 1))
    return blocks


def _peer_route_kernel(h2t_ref, wqt_ref, keys_ref, s1_ref, e1_ref, s2_ref, e2_ref, qt_s, sv_s, cand_s):
    tt = h2t_ref.shape[1]
    kk = PEER_TOPK
    lanes = 128
    qt_s[...] = jnp.dot(wqt_ref[...], h2t_ref[...], preferred_element_type=F32).astype(BF16)

    def score_step(h, carry):
        for half, s_ref in ((0, s1_ref), (1, s2_ref)):
            c = 2 * h + half
            r0 = pl.multiple_of(c * PEER_HALF, PEER_HALF)
            s_ref[h] = jnp.dot(keys_ref[c], qt_s[pl.ds(r0, PEER_HALF), :], preferred_element_type=F32)
            for lt in range(tt // lanes):
                ln = slice(lt * lanes, (lt + 1) * lanes)
                cur = s_ref[h, :, ln]
                for r in range(kk):
                    m = jnp.max(cur, axis=0, keepdims=True)
                    sv_s[c, r:r + 1, ln] = m
                    cur = jnp.where(cur == m, -jnp.inf, cur)
        return carry

    lax.fori_loop(0, PEER_HEADS, score_step, 0)

    blocks = _peer_candidate_blocks()
    sub = lax.broadcasted_iota(jnp.int32, (8, lanes), 0)

    def head_step(h, carry):
        for lt in range(tt // lanes):
            ln = slice(lt * lanes, (lt + 1) * lanes)
            sv1 = sv_s[2 * h, :, ln]
            sv2 = sv_s[2 * h + 1, :, ln]
            row = 0
            for a_lo, a_hi, n_b in blocks:
                if a_hi - a_lo == 1:
                    n_rows = -(-n_b // 8) * 8
                    blk = sv1[a_lo:a_lo + 1, :] + sv2[0:n_rows, :]
                    if n_b < n_rows:
                        blk = jnp.where(sub < n_b, blk, -jnp.inf)
                else:
                    n_rows = a_hi - a_lo
                    blk = sv1[a_lo:a_hi, :] + sv2[0:1, :]
                cand_s[row:row + n_rows, ln] = blk
                row += n_rows
            cur = cand_s[:, ln]
            m0 = sv1[0:1, :] + sv2[0:1, :]
            taken = jnp.zeros((1, lanes), F32)
            tau = m0
            zsum = jnp.zeros((1, lanes), F32)
            for r in range(kk):
                m = jnp.max(cur, axis=0, keepdims=True)
                eq = cur == m
                cnt = jnp.sum(jnp.where(eq, 1.0, 0.0), axis=0, keepdims=True)
                active = taken < kk
                tau = jnp.where(active, m, tau)
                zsum = zsum + jnp.where(active, cnt * jnp.exp(m - m0), 0.0)
                taken = taken + jnp.where(active, cnt, 0.0)
                cur = jnp.where(eq, -jnp.inf, cur)
            s1 = s1_ref[h, :, ln]
            th1 = jnp.full(s1.shape, jnp.inf, F32)
            for b in range(kk):
                th1 = jnp.where(s1 + sv2[b:b + 1, :] >= tau, sv2[b:b + 1, :], th1)
            s1_ref[h, :, ln] = th1
            e1_ref[h, :, ln] = jnp.exp(s1 - sv1[0:1, :]) / zsum
            e2_ref[h, :, ln] = jnp.exp(s2_ref[h, :, ln] - sv2[0:1, :])
        return carry

    lax.fori_loop(0, PEER_HEADS, head_step, 0)


def _peer_route(h2t, wqt, keys):
    t = h2t.shape[1]
    tt = 512
    n_cand_rows = sum((-(-n_b // 8) * 8) if a_hi - a_lo == 1 else a_hi - a_lo
                      for a_lo, a_hi, n_b in _peer_candidate_blocks())
    slab = pl.BlockSpec((PEER_HEADS, PEER_N_KEYS, tt), lambda i: (0, 0, i))
    slab_shape = jax.ShapeDtypeStruct((PEER_HEADS, PEER_N_KEYS, t), F32)
    return pl.pallas_call(
        _peer_route_kernel,
        grid=(t // tt,),
        in_specs=[pl.BlockSpec((D_MODEL, tt), lambda i: (0, i)),
                  pl.BlockSpec(wqt.shape, lambda i: (0, 0)),
                  pl.BlockSpec(keys.shape, lambda i: (0, 0, 0))],
        out_specs=[slab, slab, slab, slab],
        out_shape=[slab_shape, slab_shape, slab_shape, slab_shape],
        scratch_shapes=[pltpu.VMEM((wqt.shape[0], tt), BF16),
                        pltpu.VMEM((2 * PEER_HEADS, PEER_TOPK, tt), F32),
                        pltpu.VMEM((n_cand_rows, tt), F32)],
        compiler_params=_cp(("parallel",)),
        name="peer_route",
    )(h2t, wqt, keys)


def _gelu_exact(x):
    return 0.5 * x * (1.0 + lax.erf(x * (2.0 ** -0.5)))


PEER_ROWS_PER_TILE = 8
PEER_J_CHUNK = 32
PEER_ROW_GROUP = 8
PEER_ACT_SLAB = 32
PEER_K_SLICE = 256


def _peer_dense_kernel(h2t_ref, u_ref, vt_ref, th1_ref, e1_ref, s2_ref, e2_ref, o_ref, g_s, pre_s, act_s):
    @pl.when(pl.program_id(1) == 0)
    def _():
        o_ref[...] = jnp.zeros_like(o_ref)

    tt = h2t_ref.shape[1]
    jc = PEER_J_CHUNK
    n_k = h2t_ref.shape[0] // PEER_K_SLICE
    chunks_per_k = (tt // 128) * (PEER_N_KEYS // jc) // n_k
    pace = []
    chunk = 0
    for lt in range(tt // 128):
        ln = slice(lt * 128, (lt + 1) * 128)
        for j0 in range(0, PEER_N_KEYS, jc):
            for i0 in range(0, PEER_ROWS_PER_TILE, PEER_ROW_GROUP):
                rows = range(i0, i0 + PEER_ROW_GROUP)
                acc = {ii: None for ii in rows}
                for h in range(PEER_HEADS):
                    s2 = s2_ref[h, j0:j0 + jc, ln]
                    e2 = e2_ref[h, j0:j0 + jc, ln]
                    for ii in rows:
                        term = jnp.where(s2 >= th1_ref[h, ii:ii + 1, ln], e2, 0.0) * e1_ref[h, ii:ii + 1, ln]
                        acc[ii] = term if acc[ii] is None else acc[ii] + term
                for ii in rows:
                    sl = slice(ii * PEER_N_KEYS + j0, ii * PEER_N_KEYS + j0 + jc)
                    g_s[sl, ln] = acc[ii]
            if chunk % chunks_per_k == 0:
                whole = functools.reduce(jnp.add, [acc[ii] for ii in rows])
                pace.append(jnp.minimum(jnp.sum(whole, axis=0, keepdims=True), 0.0))
            chunk += 1

    pre = None
    for k in range(n_k):
        ks = slice(k * PEER_K_SLICE, (k + 1) * PEER_K_SLICE)
        zero = jnp.concatenate([pace[k]] * (tt // 128), axis=1).astype(BF16)
        part = jnp.dot(u_ref[:, ks], h2t_ref[ks, :] + zero, preferred_element_type=F32)
        pre = part if pre is None else pre + part
    pre_s[...] = pre
    for r0 in range(0, g_s.shape[0], PEER_ACT_SLAB):
        sl = slice(r0, r0 + PEER_ACT_SLAB)
        act_s[sl, :] = (g_s[sl, :] * _gelu_exact(pre_s[sl, :])).astype(BF16)
    o_ref[...] += jnp.dot(vt_ref[...], act_s[...], preferred_element_type=F32)


def _peer_dense(h2t, u, vt, th1, e1, s2, e2):
    t = h2t.shape[1]
    n_exp = u.shape[0]
    tt = 512
    te = PEER_ROWS_PER_TILE * PEER_N_KEYS
    rows = pl.BlockSpec((PEER_HEADS, PEER_ROWS_PER_TILE, tt), lambda i, j: (0, j, i))
    slab = pl.BlockSpec((PEER_HEADS, PEER_N_KEYS, tt), lambda i, j: (0, 0, i))
    return pl.pallas_call(
        _peer_dense_kernel,
        grid=(t // tt, n_exp // te),
        in_specs=[pl.BlockSpec((D_MODEL, tt), lambda i, j: (0, i)),
                  pl.BlockSpec((te, D_MODEL), lambda i, j: (j, 0)),
                  pl.BlockSpec((D_MODEL, te), lambda i, j: (0, j)),
                  rows, rows, slab, slab],
        out_specs=pl.BlockSpec((D_MODEL, tt), lambda i, j: (0, i)),
        out_shape=jax.ShapeDtypeStruct((D_MODEL, t), F32),
        scratch_shapes=[pltpu.VMEM((te, tt), F32), pltpu.VMEM((te, tt), F32), pltpu.VMEM((te, tt), BF16)],
        compiler_params=_cp(("parallel", "arbitrary"), vmem_mb=56),
        name="peer_dense",
    )(h2t, u, vt, th1, e1, s2, e2)


def _final_kernel(x_ref, pt_ref, g2_ref, fg_ref, oc_ref, ol_ref, *, n_ctx_tiles):
    x = x_ref[...] + g2_ref[...] * pt_ref[...].T
    y = x * lax.rsqrt(jnp.mean(x * x, axis=-1, keepdims=True) + RMS_EPS) * fg_ref[...]
    ol_ref[...] = y

    @pl.when(pl.program_id(0) < n_ctx_tiles)
    def _():
        oc_ref[...] = y


def _final(x, p_t, mod3, gate_chunk, final_g, n_ctx_rows):
    t = x.shape[0]
    tm = 256
    nct = n_ctx_rows // tm
    row = functools.partial(_mod_row, tm=tm, n_ctx_rows=n_ctx_rows)
    return pl.pallas_call(
        functools.partial(_final_kernel, n_ctx_tiles=nct),
        grid=(t // tm,),
        in_specs=[pl.BlockSpec((tm, D_MODEL), lambda i: (i, 0)),
                  pl.BlockSpec((D_MODEL, tm), lambda i: (0, i)),
                  pl.BlockSpec((None, 1, D_MODEL), lambda i: (row(i), 0, gate_chunk)),
                  pl.BlockSpec((1, D_MODEL), lambda i: (0, 0))],
        out_specs=[pl.BlockSpec((tm, D_MODEL), lambda i: (_ctx_tile(i, nct), 0)),
                   pl.BlockSpec((tm, D_MODEL), lambda i: (_lat_tile(i, nct), 0))],
        out_shape=[jax.ShapeDtypeStruct((n_ctx_rows, D_MODEL), F32),
                   jax.ShapeDtypeStruct((t - n_ctx_rows, D_MODEL), F32)],
        compiler_params=_cp(("arbitrary",)),
        name="final_norm",
    )(x, p_t, mod3, final_g)


def _group_heads(p):
    return jnp.transpose(p.reshape(2, SSD_GROUPS, HEADS_PER_GROUP), (1, 0, 2)).reshape(
        SSD_GROUPS, 2 * HEADS_PER_GROUP)


def _layer(x_parts, lp, mod3, n_ctx_batch, n_lat_batch, cache):
    n_ctx_rows = n_ctx_batch * SEQ
    t = n_ctx_rows + n_lat_batch * DEC_SEQ
    w_in = lp['w_in']
    na_w = NA_HEADS * NA_HEAD_DIM
    o_z = 3 * na_w
    o_xbc = o_z + SSD_D_INNER
    o_dt = o_xbc + SSD_CONV_CH
    o_g = o_dt + 2 * SSD_HEADS

    h1 = _norm_mod(x_parts, lp['norm1_g'][None, :], mod3, 1, 0, n_ctx_rows)
    q = _mm(h1, w_in, 0, na_w, BF16, name="in_q")
    k_new, k_lat = _mm_split(h1, w_in, na_w, na_w, F32, n_ctx_rows, name="in_k")
    v_new, v_lat = _mm_split(h1, w_in, 2 * na_w, na_w, F32, n_ctx_rows, name="in_v")
    z = _mm(h1, w_in, o_z, SSD_D_INNER, F32, name="in_z")
    xbc = _mm(h1, w_in, o_xbc, SSD_CONV_CH, F32, name="in_xbc")
    dt_raw = _mm(h1, w_in, o_dt, 2 * SSD_HEADS, F32, name="in_dt")
    gates = _mm(h1, w_in[:, o_g:], 0, 2 * D_MODEL, F32, name="in_gates")

    k_ctx, v_ctx, init_f, init_b = cache
    na_ctx = _ctx_attention(q, k_new, v_new, n_ctx_batch)
    na_lat = _na_latent(q, k_lat, v_lat, k_ctx.reshape(n_lat_batch, -1, na_w),
                        v_ctx.reshape(n_lat_batch, -1, na_w), lp['na_rpb'], n_ctx_rows, n_lat_batch)

    dtg = jnp.transpose(dt_raw.reshape(t, 2, SSD_GROUPS, HEADS_PER_GROUP), (2, 0, 1, 3)).reshape(
        SSD_GROUPS, t, 2 * HEADS_PER_GROUP)
    dtgt = jnp.transpose(dtg, (0, 2, 1))
    conv_w8 = jnp.concatenate([lp['conv_w'], jnp.zeros((8 - SSD_CONV, SSD_CONV_CH), F32)], axis=0)
    dtb = _group_heads(lp['ssd_dt_bias'])
    alog = _group_heads(lp['ssd_a_log'])
    dexp = jnp.repeat(lp['ssd_d'], SSD_HEAD_DIM)[None, :]
    ssd_args = (xbc, z, dtg, dtgt, conv_w8, lp['conv_b'][None, :], dtb[:, None, :], dtb[:, :, None],
                alog[:, None, :], alog[:, :, None], dexp, lp['ssd_norm_g'][None, :])
    ssd_ctx, s_f, s_b = _ssd(*ssd_args, row_start=0, nb=n_ctx_batch, seq=SEQ)
    ssd_lat = _ssd(*ssd_args, row_start=n_ctx_rows, nb=n_lat_batch, seq=DEC_SEQ, init=(init_f, init_b))

    mixed = _mix((na_ctx, na_lat), (ssd_ctx, ssd_lat), lp['w_na_proj'].astype(BF16),
                 lp['w_ssd_proj'].astype(BF16), gates, n_ctx_rows)
    x1 = _outproj_residual(mixed, lp['w_out'].astype(BF16), x_parts, mod3, 2, n_ctx_rows)

    h2t = _norm_mod((x1,), lp['norm2_g'][None, :], mod3, 4, 3, n_ctx_rows, transpose_out=True)
    keys = lp['peer_keys'].reshape(2 * PEER_HEADS, PEER_N_KEYS, PEER_HALF).astype(BF16)
    th1, e1, s2, e2 = _peer_route(h2t, lp['peer_wq'].T.astype(BF16), keys)
    peer_t = _peer_dense(h2t, lp['peer_u'].astype(BF16), lp['peer_v'].T.astype(BF16), th1, e1, s2, e2)
    new_k = k_new.reshape(n_ctx_batch, SEQ, NA_HEADS, NA_HEAD_DIM)
    new_v = v_new.reshape(n_ctx_batch, SEQ, NA_HEADS, NA_HEAD_DIM)
    return x1, peer_t, (new_k, new_v, s_f, s_b)


def kernel(x_prompt, x_sample, c, c_ctx, cache_na_k, cache_na_v, state_ssd_fwd, state_ssd_bwd, ada_w, ada_b, norm1_g, norm2_g, w_in, conv_w, conv_b, na_rpb, ssd_a_log, ssd_dt_bias, ssd_d, ssd_norm_g, w_na_proj, w_ssd_proj, w_out, peer_wq, peer_keys, peer_u, peer_v, final_g):
    n_ctx_batch = x_prompt.shape[0]
    n_lat_batch = x_sample.shape[0]
    depth = ada_w.shape[0]
    n_ctx_rows = n_ctx_batch * SEQ
    assert x_prompt.shape[1] == SEQ and x_sample.shape[1] == DEC_SEQ
    assert 1 + n_lat_batch <= MOD_ROWS and n_ctx_rows % DEC_SEQ == 0
    assert depth == 1, "the final-norm kernel folds in the PEER residual of the single trunk layer"

    x_parts = (x_prompt.reshape(-1, D_MODEL), x_sample.reshape(-1, D_MODEL))
    cvec = jnp.concatenate([c_ctx[None, :], c, jnp.zeros((MOD_ROWS - 1 - n_lat_batch, D_MODEL), F32)], axis=0)
    lp = {'norm1_g': norm1_g[0], 'norm2_g': norm2_g[0], 'w_in': w_in[0], 'conv_w': conv_w[0],
          'conv_b': conv_b[0], 'na_rpb': na_rpb[0], 'ssd_a_log': ssd_a_log[0],
          'ssd_dt_bias': ssd_dt_bias[0], 'ssd_d': ssd_d[0], 'ssd_norm_g': ssd_norm_g[0],
          'w_na_proj': w_na_proj[0], 'w_ssd_proj': w_ssd_proj[0], 'w_out': w_out[0],
          'peer_wq': peer_wq[0], 'peer_keys': peer_keys[0], 'peer_u': peer_u[0], 'peer_v': peer_v[0]}
    mod3 = _ada(cvec, ada_w[0], ada_b[0][None, :])[:, None, :]
    cache = (cache_na_k[:, 0], cache_na_v[:, 0], state_ssd_fwd[:, 0], state_ssd_bwd[:, 0])
    x1, peer_t, (new_k, new_v, new_f, new_b) = _layer(x_parts, lp, mod3, n_ctx_batch, n_lat_batch, cache)
    y_ctx, y_lat = _final(x1, peer_t, mod3, 5, final_g[None, :], n_ctx_rows)
    return (y_ctx.reshape(x_prompt.shape), y_lat.reshape(x_sample.shape),
            new_k[:, None], new_v[:, None], new_f[:, None], new_b[:, None])
```

```python
import functools

import jax
import jax.numpy as jnp
from jax import lax
from jax.experimental import pallas as pl
from jax.experimental.pallas import tpu as pltpu

F32 = jnp.float32
BF16 = jnp.bfloat16

D_MODEL = 2048
SEQ = 256
DEC_SEQ = 1024
GRID_W = 64
NA_HEADS = 16
NA_HEAD_DIM = 128
NA_WIN_ROWS = 8
NA_WIN_COLS = 16
SSD_D_INNER = 4096
SSD_HEAD_DIM = 64
SSD_HEADS = 64
SSD_GROUPS = 8
SSD_STATE = 128
SSD_CONV = 5
SSD_CHUNK = 128
SSD_CONV_CH = SSD_D_INNER + 2 * SSD_GROUPS * SSD_STATE
HEADS_PER_GROUP = SSD_HEADS // SSD_GROUPS
GROUP_CH = SSD_D_INNER // SSD_GROUPS
PEER_HEADS = 8
PEER_HALF = 128
PEER_N_KEYS = 128
PEER_TOPK = 16
RMS_EPS = 1e-6
NEG_BIG = -1e30
MOD_ROWS = 8
VMEM_LIMIT_MB = 48

NT_DIMS = (((1,), (1,)), ((), ()))


def _cp(sem, vmem_mb=VMEM_LIMIT_MB, flags=None):
    return pltpu.CompilerParams(dimension_semantics=sem, vmem_limit_bytes=vmem_mb * 1024 * 1024, flags=flags)


def _mod_row(i, tm, n_ctx_rows):
    n_ctx_tiles = n_ctx_rows // tm
    return jnp.where(i < n_ctx_tiles, 0, 1 + (i - n_ctx_tiles) // (DEC_SEQ // tm))


def _silu(x):
    return x * jax.nn.sigmoid(x)


def _softplus(x):
    return jnp.maximum(x, 0.0) + jnp.log1p(jnp.exp(-jnp.abs(x)))


def _split3(x):
    hi = x.astype(BF16)
    r1 = x - hi.astype(F32)
    mid = r1.astype(BF16)
    lo = (r1 - mid.astype(F32)).astype(BF16)
    return hi, mid, lo


def _dot_f32_by_01(x, m01):
    return sum(jnp.dot(p, m01, preferred_element_type=F32) for p in _split3(x))


def _dot_01_by_f32(m01, x):
    return sum(jnp.dot(m01, p, preferred_element_type=F32) for p in _split3(x))


def _ada_kernel(c_ref, w_ref, b_ref, o_ref):
    s = _silu(c_ref[...])
    o_ref[...] = jnp.dot(s, w_ref[...], preferred_element_type=F32,
                         precision=lax.Precision.HIGHEST) + b_ref[...]


def _ada(cvec, ada_w, ada_b):
    n = ada_w.shape[1]
    tn = 1024
    return pl.pallas_call(
        _ada_kernel,
        grid=(n // tn,),
        in_specs=[pl.BlockSpec((MOD_ROWS, D_MODEL), lambda j: (0, 0)),
                  pl.BlockSpec((D_MODEL, tn), lambda j: (0, j)),
                  pl.BlockSpec((1, tn), lambda j: (0, j))],
        out_specs=pl.BlockSpec((MOD_ROWS, tn), lambda j: (0, j)),
        out_shape=jax.ShapeDtypeStruct((MOD_ROWS, n), F32),
        compiler_params=_cp(("parallel",)),
        name="ada_mod",
    )(cvec, ada_w, ada_b)


def _norm_mod_kernel(*refs, transpose_out, n_ctx_tiles):
    if n_ctx_tiles is None:
        x_ref, g_ref, sc_ref, sh_ref, o_ref = refs
        x = x_ref[...]
    else:
        xc_ref, xl_ref, g_ref, sc_ref, sh_ref, o_ref = refs
        x = jnp.where(pl.program_id(0) < n_ctx_tiles, xc_ref[...], xl_ref[...])
    y = x * lax.rsqrt(jnp.mean(x * x, axis=-1, keepdims=True) + RMS_EPS) * g_ref[...]
    y = y * (1.0 + sc_ref[...]) + sh_ref[...]
    o_ref[...] = (y.T if transpose_out else y).astype(o_ref.dtype)


def _norm_mod(x_parts, g, mod3, sc_chunk, sh_chunk, n_ctx_rows, transpose_out=False):
    t = sum(p.shape[0] for p in x_parts)
    tm = 256
    nct = n_ctx_rows // tm
    row = functools.partial(_mod_row, tm=tm, n_ctx_rows=n_ctx_rows)
    if len(x_parts) == 1:
        x_specs = [pl.BlockSpec((tm, D_MODEL), lambda i: (i, 0))]
    else:
        x_specs = [pl.BlockSpec((tm, D_MODEL), lambda i: (_ctx_tile(i, nct), 0)),
                   pl.BlockSpec((tm, D_MODEL), lambda i: (_lat_tile(i, nct), 0))]
    if transpose_out:
        out_spec = pl.BlockSpec((D_MODEL, tm), lambda i: (0, i))
        out_shape = jax.ShapeDtypeStruct((D_MODEL, t), BF16)
    else:
        out_spec = pl.BlockSpec((tm, D_MODEL), lambda i: (i, 0))
        out_shape = jax.ShapeDtypeStruct((t, D_MODEL), BF16)
    return pl.pallas_call(
        functools.partial(_norm_mod_kernel, transpose_out=transpose_out,
                          n_ctx_tiles=None if len(x_parts) == 1 else nct),
        grid=(t // tm,),
        in_specs=x_specs + [pl.BlockSpec((1, D_MODEL), lambda i: (0, 0)),
                            pl.BlockSpec((None, 1, D_MODEL), lambda i: (row(i), 0, sc_chunk)),
                            pl.BlockSpec((None, 1, D_MODEL), lambda i: (row(i), 0, sh_chunk))],
        out_specs=out_spec,
        out_shape=out_shape,
        compiler_params=_cp(("parallel",)),
        name="norm_mod",
    )(*x_parts, g, mod3, mod3)


def _mm_kernel(a_ref, w_ref, o_ref, wb_s):
    @pl.when(pl.program_id(1) == 0)
    def _():
        wb_s[...] = w_ref[...].astype(BF16)

    o_ref[...] = jnp.dot(a_ref[...], wb_s[...], preferred_element_type=F32).astype(o_ref.dtype)


def _mm(a, w, col0, n, out_dtype, tm=1024, tn=1024, name="mm"):
    m, k = a.shape
    tn = min(tn, n)
    assert col0 % tn == 0 and n % tn == 0 and m % tm == 0
    c0 = col0 // tn
    return pl.pallas_call(
        _mm_kernel,
        grid=(n // tn, m // tm),
        in_specs=[pl.BlockSpec((tm, k), lambda j, i: (i, 0)),
                  pl.BlockSpec((k, tn), lambda j, i: (0, c0 + j))],
        out_specs=pl.BlockSpec((tm, tn), lambda j, i: (i, j)),
        out_shape=jax.ShapeDtypeStruct((m, n), out_dtype),
        scratch_shapes=[pltpu.VMEM((k, tn), BF16)],
        compiler_params=_cp(("arbitrary", "arbitrary"), vmem_mb=56),
        name=name,
    )(a, w)


def _ctx_tile(i, n_ctx_tiles):
    return jnp.minimum(i, n_ctx_tiles - 1)


def _lat_tile(i, n_ctx_tiles):
    return jnp.maximum(i - n_ctx_tiles, 0)


def _mm_split_kernel(a_ref, w_ref, oc_ref, ol_ref, wb_s, *, n_ctx_tiles):
    @pl.when(pl.program_id(1) == 0)
    def _():
        wb_s[...] = w_ref[...].astype(BF16)

    y = jnp.dot(a_ref[...], wb_s[...], preferred_element_type=F32)
    ol_ref[...] = y.astype(ol_ref.dtype)

    @pl.when(pl.program_id(1) < n_ctx_tiles)
    def _():
        oc_ref[...] = y.astype(oc_ref.dtype)


def _mm_split(a, w, col0, n, out_dtype, n_ctx_rows, tm=1024, tn=1024, name="mm_split"):
    m, k = a.shape
    assert col0 % tn == 0 and n % tn == 0 and m % tm == 0 and n_ctx_rows % tm == 0
    nct = n_ctx_rows // tm
    c0 = col0 // tn
    return pl.pallas_call(
        functools.partial(_mm_split_kernel, n_ctx_tiles=nct),
        grid=(n // tn, m // tm),
        in_specs=[pl.BlockSpec((tm, k), lambda j, i: (i, 0)),
                  pl.BlockSpec((k, tn), lambda j, i: (0, c0 + j))],
        out_specs=[pl.BlockSpec((tm, tn), lambda j, i: (_ctx_tile(i, nct), j)),
                   pl.BlockSpec((tm, tn), lambda j, i: (_lat_tile(i, nct), j))],
        out_shape=[jax.ShapeDtypeStruct((n_ctx_rows, n), out_dtype),
                   jax.ShapeDtypeStruct((m - n_ctx_rows, n), out_dtype)],
        scratch_shapes=[pltpu.VMEM((k, tn), BF16)],
        compiler_params=_cp(("arbitrary", "arbitrary"), vmem_mb=56),
        name=name,
    )(a, w)


def _ctx_attn_kernel(q_ref, k_ref, v_ref, o_ref):
    scale = NA_HEAD_DIM ** -0.5
    for h in range(NA_HEADS):
        sl = slice(h * NA_HEAD_DIM, (h + 1) * NA_HEAD_DIM)
        q = q_ref[:, sl]
        k = k_ref[:, sl].astype(BF16)
        v = v_ref[:, sl].astype(BF16)
        s = lax.dot_general(q, k, NT_DIMS, preferred_element_type=F32) * scale
        p = jnp.exp(s - jnp.max(s, axis=-1, keepdims=True))
        l = jnp.sum(p, axis=-1, keepdims=True)
        o = jnp.dot(p.astype(BF16), v, preferred_element_type=F32) / l
        o_ref[:, sl] = o.astype(o_ref.dtype)


def _ctx_attention(q, k, v, n_ctx_batch):
    width = NA_HEADS * NA_HEAD_DIM
    spec = pl.BlockSpec((SEQ, width), lambda b: (b, 0))
    return pl.pallas_call(
        _ctx_attn_kernel,
        grid=(n_ctx_batch,),
        in_specs=[spec, spec, spec],
        out_specs=spec,
        out_shape=jax.ShapeDtypeStruct((n_ctx_batch * SEQ, width), BF16),
        compiler_params=_cp(("parallel",)),
        name="ctx_attn",
    )(q, k, v)


def _na_lat_kernel(q_ref, k_ref, v_ref, kc_ref, vc_ref, rpb_ref, o_ref, pair_s, sc_s, pc_s, ow_s):
    scale = NA_HEAD_DIM ** -0.5
    rows = DEC_SEQ // GRID_W
    kc = kc_ref[...].astype(BF16)
    vc = vc_ref[...].astype(BF16)

    two_w = 2 * GRID_W
    qi = lax.broadcasted_iota(jnp.int32, (GRID_W, two_w), 0)
    lane = lax.broadcasted_iota(jnp.int32, (GRID_W, two_w), 1)
    kj = lane & (GRID_W - 1)
    col_start = jnp.clip(qi - NA_WIN_COLS // 2, 0, GRID_W - NA_WIN_COLS)
    in_window = (kj >= col_start) & (kj < col_start + NA_WIN_COLS)
    first_half = lane < GRID_W
    n_off = 2 * NA_WIN_ROWS - 1

    def toeplitz(d, lane_off):
        row = jnp.broadcast_to(rpb_ref[d:d + 1, :], (GRID_W, two_w))
        return pltpu.roll(row, (lane_off - (NA_WIN_COLS - 1)) % two_w, 1, stride=1, stride_axis=0)

    for d in range(n_off - 1):
        both = jnp.where(first_half, toeplitz(d, 0), toeplitz(d + 1, GRID_W))
        pair_s[d] = jnp.where(in_window, both, NEG_BIG)

    sc_s[...] = lax.dot_general(q_ref[...], kc, NT_DIMS, preferred_element_type=F32) * scale

    def row_start(r):
        return min(max(r - NA_WIN_ROWS // 2, 0), rows - NA_WIN_ROWS)

    groups = []
    for r in range(rows):
        if groups and row_start(groups[-1][0]) == row_start(r):
            groups[-1].append(r)
        else:
            groups.append([r])
    for grp in groups:
        rs = row_start(grp[0])
        qs = slice(grp[0] * GRID_W, (grp[-1] + 1) * GRID_W)
        win = slice(rs * GRID_W, (rs + NA_WIN_ROWS) * GRID_W)
        kw = k_ref[win, :].astype(BF16)
        vw = v_ref[win, :].astype(BF16)
        sw_all = lax.dot_general(q_ref[qs, :], kw, NT_DIMS, preferred_element_type=F32) * scale
        pws, inv_ls = [], []
        for n, r in enumerate(grp):
            rq = slice(r * GRID_W, (r + 1) * GRID_W)
            bias = jnp.concatenate([pair_s[rs + 2 * kp - r + NA_WIN_ROWS - 1] for kp in range(NA_WIN_ROWS // 2)],
                                   axis=1)
            sw = sw_all[n * GRID_W:(n + 1) * GRID_W, :] + bias
            sc = sc_s[rq, :]
            m = jnp.maximum(jnp.max(sw, axis=-1, keepdims=True), jnp.max(sc, axis=-1, keepdims=True))
            pw = jnp.exp(sw - m)
            pc = jnp.exp(sc - m)
            inv_l = 1.0 / (jnp.sum(pw, axis=-1, keepdims=True) + jnp.sum(pc, axis=-1, keepdims=True))
            pc_s[rq, :] = (pc * inv_l).astype(BF16)
            pws.append(pw.astype(BF16))
            inv_ls.append(inv_l)
        ow = jnp.dot(jnp.concatenate(pws, axis=0), vw, preferred_element_type=F32)
        ow_s[qs, :] = ow * jnp.concatenate(inv_ls, axis=0)
    o = ow_s[...] + jnp.dot(pc_s[...], vc, preferred_element_type=F32)
    o_ref[...] = o.astype(o_ref.dtype)


def _na_latent(q, k, v, k_ctx, v_ctx, rpb, n_ctx_rows, n_lat_batch):
    off = n_ctx_rows // DEC_SEQ
    n_off = 2 * NA_WIN_ROWS - 1
    rpb_pad = jnp.zeros((NA_HEADS, 16, 2 * GRID_W), F32).at[:, :n_off, :2 * NA_WIN_COLS - 1].set(rpb)
    q_tok = pl.BlockSpec((DEC_SEQ, NA_HEAD_DIM), lambda b, h: (off + b, h))
    tok = pl.BlockSpec((DEC_SEQ, NA_HEAD_DIM), lambda b, h: (b, h))
    ctx = pl.BlockSpec((None, k_ctx.shape[1], NA_HEAD_DIM), lambda b, h: (b, 0, h))
    return pl.pallas_call(
        _na_lat_kernel,
        grid=(n_lat_batch, NA_HEADS),
        in_specs=[q_tok, tok, tok, ctx, ctx,
                  pl.BlockSpec((None, 16, 2 * GRID_W), lambda b, h: (h, 0, 0))],
        out_specs=pl.BlockSpec((DEC_SEQ, NA_HEAD_DIM), lambda b, h: (b, h)),
        out_shape=jax.ShapeDtypeStruct((n_lat_batch * DEC_SEQ, NA_HEADS * NA_HEAD_DIM), BF16),
        scratch_shapes=[pltpu.VMEM((n_off - 1, GRID_W, 2 * GRID_W), F32),
                        pltpu.VMEM((DEC_SEQ, k_ctx.shape[1]), F32),
                        pltpu.VMEM((DEC_SEQ, k_ctx.shape[1]), BF16),
                        pltpu.VMEM((DEC_SEQ, NA_HEAD_DIM), F32)],
        compiler_params=_cp(("parallel", "parallel")),
        name="na_latent",
    )(q, k, v, k_ctx, v_ctx, rpb_pad)


def _ssd_kernel(*refs, seq, has_init):
    n_in = 20 if has_init else 18
    (xs_ref, b_ref, c_ref, z_ref, dt_ref, dtt_ref, cwx_ref, cwb_ref, cwc_ref,
     cbx_ref, cbb_ref, cbc_ref, dtb_ref, dtbt_ref, al_ref, alt_ref, dexp_ref, ng_ref) = refs[:18]
    if has_init:
        if_ref, ib_ref = refs[18:20]
        (y_ref,) = refs[n_in:n_in + 1]
        scratch = refs[n_in + 1:]
    else:
        y_ref, sfo_ref, sbo_ref = refs[n_in:n_in + 3]
        scratch = refs[n_in + 3:]
    pad_s, xs_s, bb_s, cc_s, dt_s, cumc_s, cumr_s, dtr_s, ddt_s, dtot_s, yacc_s, sf_s, sb_s = scratch

    nc = seq // SSD_CHUNK
    ck = SSD_CHUNK
    halo = 8

    def conv_silu(dst_ref, src_ref, w_ref, bias_ref, width):
        pad_s[0:halo, 0:width] = jnp.zeros((halo, width), F32)
        pad_s[seq + halo:seq + 2 * halo, 0:width] = jnp.zeros((halo, width), F32)
        pad_s[halo:seq + halo, 0:width] = src_ref[...]
        for ci in range(nc):
            base = halo - SSD_CONV // 2 + ci * ck
            acc = bias_ref[...] + pad_s[base:base + ck, 0:width] * w_ref[0:1, :]
            for tap in range(1, SSD_CONV):
                acc = acc + pad_s[base + tap:base + tap + ck, 0:width] * w_ref[tap:tap + 1, :]
            dst_ref[ci * ck:(ci + 1) * ck, :] = _silu(acc)

    conv_silu(xs_s, xs_ref, cwx_ref, cbx_ref, GROUP_CH)
    conv_silu(bb_s, b_ref, cwb_ref, cbb_ref, SSD_STATE)
    conv_silu(cc_s, c_ref, cwc_ref, cbc_ref, SSD_STATE)

    li = lax.broadcasted_iota(jnp.int32, (ck, ck), 0)
    si = lax.broadcasted_iota(jnp.int32, (ck, ck), 1)
    lower = li >= si
    upper = si >= li
    tri_lo = jnp.where(lower, 1.0, 0.0).astype(BF16)
    tri_up = jnp.where(upper, 1.0, 0.0).astype(BF16)
    nh2 = 2 * HEADS_PER_GROUP

    dt_s[...] = _softplus(dt_ref[...] + dtb_ref[...])
    a_row = -jnp.exp(al_ref[...])
    dtt = _softplus(dtt_ref[...] + dtbt_ref[...])
    dat = dtt * (-jnp.exp(alt_ref[...]))
    col_is_fwd = lax.broadcasted_iota(jnp.int32, (ck, nh2), 1) < HEADS_PER_GROUP
    row_is_fwd = lax.broadcasted_iota(jnp.int32, (nh2, ck), 0) < HEADS_PER_GROUP
    for c in range(nc):
        da = dt_s[c * ck:(c + 1) * ck, :] * a_row
        cumc_s[c] = jnp.where(col_is_fwd, _dot_01_by_f32(tri_lo, da), _dot_01_by_f32(tri_up, da))
        dat_c = dat[:, c * ck:(c + 1) * ck]
        cumr = jnp.where(row_is_fwd, _dot_f32_by_01(dat_c, tri_up), _dot_f32_by_01(dat_c, tri_lo))
        cumr_s[c] = cumr
        total = jnp.where(row_is_fwd, jnp.broadcast_to(cumr[:, ck - 1:ck], (nh2, ck)),
                          jnp.broadcast_to(cumr[:, 0:1], (nh2, ck)))
        dtt_c = dtt[:, c * ck:(c + 1) * ck]
        dtr_s[c] = dtt_c
        ddt_s[c] = jnp.exp(total - cumr) * dtt_c
        dtot_s[c] = jnp.exp(total)

    lane_lo = lax.broadcasted_iota(jnp.int32, (ck, 2 * SSD_HEAD_DIM), 1) < SSD_HEAD_DIM

    yacc_s[...] = jnp.zeros_like(yacc_s)
    if has_init:
        sf_s[...] = if_ref[...].reshape(GROUP_CH, SSD_STATE).T
        sb_s[...] = ib_ref[...].reshape(GROUP_CH, SSD_STATE).T
    else:
        sf_s[...] = jnp.zeros_like(sf_s)
        sb_s[...] = jnp.zeros_like(sb_s)

    def chunk_step(i, carry):
        for d in (0, 1):
            c = i if d == 0 else nc - 1 - i
            r0 = pl.multiple_of(c * ck, ck)
            x_bf = xs_s[pl.ds(r0, ck), :].astype(BF16)
            b_c = bb_s[pl.ds(r0, ck), :]
            c_f = cc_s[pl.ds(r0, ck), :]
            cumc = cumc_s[c]
            cumr = cumr_s[c]
            dtr = dtr_s[c]
            ddt = ddt_s[c]
            dtot = dtot_s[c]
            state_ref = sf_s if d == 0 else sb_s
            tri_mask = lower if d == 0 else upper

            cb = lax.dot_general(c_f.astype(BF16), b_c.astype(BF16), NT_DIMS, preferred_element_type=F32)
            bt = b_c.T
            state = state_ref[...]
            s_bf = state.astype(BF16)

            y_pieces, st_pieces = [], []
            for hp in range(HEADS_PER_GROUP // 2):
                cols = slice(hp * 2 * SSD_HEAD_DIM, (hp + 1) * 2 * SSD_HEAD_DIM)
                x_pair = x_bf[:, cols]
                rhs = jnp.concatenate([x_pair, s_bf[:, cols]], axis=0)
                ys, sts, dts = [], [], []
                for sub in range(2):
                    j = d * HEADS_PER_GROUP + hp * 2 + sub
                    cum_l = jnp.broadcast_to(cumc[:, j:j + 1], (ck, ck))
                    seg = cum_l - cumr[j:j + 1, :]
                    lmat = jnp.where(tri_mask, jnp.exp(seg), 0.0)
                    lhs = jnp.concatenate([cb * lmat * dtr[j:j + 1, :], c_f * jnp.exp(cum_l)], axis=1)
                    ys.append(jnp.dot(lhs.astype(BF16), rhs, preferred_element_type=F32))
                    w = (bt * ddt[j:j + 1, :]).astype(BF16)
                    sts.append(jnp.dot(w, x_pair, preferred_element_type=F32))
                    dts.append(dtot[j:j + 1, :])
                y_pieces.append(jnp.where(lane_lo, ys[0], ys[1]))
                decay = jnp.where(lane_lo[0:1, :], dts[0], dts[1])
                st_pieces.append(state[:, cols] * decay + jnp.where(lane_lo, sts[0], sts[1]))
            state_ref[...] = jnp.concatenate(st_pieces, axis=1)
            yacc_s[pl.ds(r0, ck), :] = yacc_s[pl.ds(r0, ck), :] + jnp.concatenate(y_pieces, axis=1)
        return carry

    lax.fori_loop(0, nc, chunk_step, 0)

    for ci in range(nc):
        rows = slice(ci * ck, (ci + 1) * ck)
        zz = z_ref[rows, :]
        y = (yacc_s[rows, :] + dexp_ref[...] * xs_s[rows, :]) * _silu(zz)
        y = y * lax.rsqrt(jnp.mean(y * y, axis=-1, keepdims=True) + RMS_EPS) * ng_ref[...]
        y_ref[rows, :] = y.astype(y_ref.dtype)

    if not has_init:
        sfo_ref[...] = sf_s[...].T.reshape(HEADS_PER_GROUP, SSD_HEAD_DIM, SSD_STATE)
        sbo_ref[...] = sb_s[...].T.reshape(HEADS_PER_GROUP, SSD_HEAD_DIM, SSD_STATE)


def _ssd(xbc, z, dtg, dtgt, conv_w8, conv_b, dtb, dtbt, alog, alogt, dexp, norm_g,
         row_start, nb, seq, init=None):
    off = row_start // seq
    g_b = SSD_D_INNER // SSD_STATE
    g_c = g_b + SSD_GROUPS
    has_init = init is not None
    in_specs = [
        pl.BlockSpec((seq, GROUP_CH), lambda b, g: (off + b, g)),
        pl.BlockSpec((seq, SSD_STATE), lambda b, g: (off + b, g_b + g)),
        pl.BlockSpec((seq, SSD_STATE), lambda b, g: (off + b, g_c + g)),
        pl.BlockSpec((seq, GROUP_CH), lambda b, g: (off + b, g)),
        pl.BlockSpec((None, seq, 2 * HEADS_PER_GROUP), lambda b, g: (g, off + b, 0)),
        pl.BlockSpec((None, 2 * HEADS_PER_GROUP, seq), lambda b, g: (g, 0, off + b)),
        pl.BlockSpec((8, GROUP_CH), lambda b, g: (0, g)),
        pl.BlockSpec((8, SSD_STATE), lambda b, g: (0, g_b + g)),
        pl.BlockSpec((8, SSD_STATE), lambda b, g: (0, g_c + g)),
        pl.BlockSpec((1, GROUP_CH), lambda b, g: (0, g)),
        pl.BlockSpec((1, SSD_STATE), lambda b, g: (0, g_b + g)),
        pl.BlockSpec((1, SSD_STATE), lambda b, g: (0, g_c + g)),
        pl.BlockSpec((None, 1, 2 * HEADS_PER_GROUP), lambda b, g: (g, 0, 0)),
        pl.BlockSpec((None, 2 * HEADS_PER_GROUP, 1), lambda b, g: (g, 0, 0)),
        pl.BlockSpec((None, 1, 2 * HEADS_PER_GROUP), lambda b, g: (g, 0, 0)),
        pl.BlockSpec((None, 2 * HEADS_PER_GROUP, 1), lambda b, g: (g, 0, 0)),
        pl.BlockSpec((1, GROUP_CH), lambda b, g: (0, g)),
        pl.BlockSpec((1, GROUP_CH), lambda b, g: (0, g)),
    ]
    args = [xbc, xbc, xbc, z, dtg, dtgt, conv_w8, conv_w8, conv_w8, conv_b, conv_b, conv_b,
            dtb, dtbt, alog, alogt, dexp, norm_g]
    state_spec = pl.BlockSpec((None, HEADS_PER_GROUP, SSD_HEAD_DIM, SSD_STATE), lambda b, g: (b, g, 0, 0))
    y_spec = pl.BlockSpec((seq, GROUP_CH), lambda b, g: (b, g))
    y_shape = jax.ShapeDtypeStruct((nb * seq, SSD_D_INNER), BF16)
    if has_init:
        in_specs += [state_spec, state_spec]
        args += list(init)
        out_specs = y_spec
        out_shape = y_shape
    else:
        st_shape = jax.ShapeDtypeStruct((nb, SSD_HEADS, SSD_HEAD_DIM, SSD_STATE), F32)
        out_specs = [y_spec, state_spec, state_spec]
        out_shape = [y_shape, st_shape, st_shape]
    nc = seq // SSD_CHUNK
    scratch = [
        pltpu.VMEM((seq + 16, GROUP_CH), F32),
        pltpu.VMEM((seq, GROUP_CH), F32),
        pltpu.VMEM((seq, SSD_STATE), F32),
        pltpu.VMEM((seq, SSD_STATE), F32),
        pltpu.VMEM((seq, 2 * HEADS_PER_GROUP), F32),
        pltpu.VMEM((nc, SSD_CHUNK, 2 * HEADS_PER_GROUP), F32),
        pltpu.VMEM((nc, 2 * HEADS_PER_GROUP, SSD_CHUNK), F32),
        pltpu.VMEM((nc, 2 * HEADS_PER_GROUP, SSD_CHUNK), F32),
        pltpu.VMEM((nc, 2 * HEADS_PER_GROUP, SSD_CHUNK), F32),
        pltpu.VMEM((nc, 2 * HEADS_PER_GROUP, SSD_CHUNK), F32),
        pltpu.VMEM((seq, GROUP_CH), F32),
        pltpu.VMEM((SSD_STATE, GROUP_CH), F32),
        pltpu.VMEM((SSD_STATE, GROUP_CH), F32),
    ]
    return pl.pallas_call(
        functools.partial(_ssd_kernel, seq=seq, has_init=has_init),
        grid=(nb, SSD_GROUPS),
        in_specs=in_specs,
        out_specs=out_specs,
        out_shape=out_shape,
        scratch_shapes=scratch,
        compiler_params=_cp(("parallel", "parallel")),
        name="ssd_lat" if has_init else "ssd_ctx",
    )(*args)


def _mix_kernel(nac_ref, nal_ref, ssdc_ref, ssdl_ref, wna_ref, wssd_ref, gna_ref, gssd_ref, o_ref, *, n_ctx_tiles):
    is_ctx = pl.program_id(0) < n_ctx_tiles
    na = jnp.where(is_ctx, nac_ref[...], nal_ref[...])
    ssd = jnp.where(is_ctx, ssdc_ref[...], ssdl_ref[...])
    a = jnp.dot(na, wna_ref[...], preferred_element_type=F32)
    s = jnp.dot(ssd, wssd_ref[...], preferred_element_type=F32)
    o = jax.nn.sigmoid(gna_ref[...]) * a + jax.nn.sigmoid(gssd_ref[...]) * s
    o_ref[...] = o.astype(o_ref.dtype)


def _mix(na_parts, ssd_parts, w_na, w_ssd, gates, n_ctx_rows):
    t = gates.shape[0]
    tm, tn = 512, 512
    nj = D_MODEL // tn
    nct = n_ctx_rows // tm
    ctx_rows = lambda i, j: (_ctx_tile(i, nct), 0)
    lat_rows = lambda i, j: (_lat_tile(i, nct), 0)
    return pl.pallas_call(
        functools.partial(_mix_kernel, n_ctx_tiles=nct),
        grid=(t // tm, nj),
        in_specs=[pl.BlockSpec((tm, w_na.shape[0]), ctx_rows),
                  pl.BlockSpec((tm, w_na.shape[0]), lat_rows),
                  pl.BlockSpec((tm, w_ssd.shape[0]), ctx_rows),
                  pl.BlockSpec((tm, w_ssd.shape[0]), lat_rows),
                  pl.BlockSpec((w_na.shape[0], tn), lambda i, j: (0, j)),
                  pl.BlockSpec((w_ssd.shape[0], tn), lambda i, j: (0, j)),
                  pl.BlockSpec((tm, tn), lambda i, j: (i, j)),
                  pl.BlockSpec((tm, tn), lambda i, j: (i, nj + j))],
        out_specs=pl.BlockSpec((tm, tn), lambda i, j: (i, j)),
        out_shape=jax.ShapeDtypeStruct((t, D_MODEL), BF16),
        compiler_params=_cp(("parallel", "parallel"), vmem_mb=56),
        name="branch_mix",
    )(*na_parts, *ssd_parts, w_na, w_ssd, gates, gates)


def _outproj_kernel(a_ref, w_ref, xc_ref, xl_ref, g_ref, o_ref, *, n_ctx_tiles):
    y = jnp.dot(a_ref[...], w_ref[...], preferred_element_type=F32)
    x = jnp.where(pl.program_id(0) < n_ctx_tiles, xc_ref[...], xl_ref[...])
    o_ref[...] = x + g_ref[...] * y


def _outproj_residual(a, w, x_parts, mod3, gate_chunk, n_ctx_rows):
    t = a.shape[0]
    tm, tn = 1024, 1024
    nj = D_MODEL // tn
    nct = n_ctx_rows // tm
    row = functools.partial(_mod_row, tm=tm, n_ctx_rows=n_ctx_rows)
    return pl.pallas_call(
        functools.partial(_outproj_kernel, n_ctx_tiles=nct),
        grid=(t // tm, nj),
        in_specs=[pl.BlockSpec((tm, a.shape[1]), lambda i, j: (i, 0)),
                  pl.BlockSpec((w.shape[0], tn), lambda i, j: (0, j)),
                  pl.BlockSpec((tm, tn), lambda i, j: (_ctx_tile(i, nct), jnp.where(i < nct, j, nj - 1))),
                  pl.BlockSpec((tm, tn), lambda i, j: (_lat_tile(i, nct), jnp.where(i < nct, 0, j))),
                  pl.BlockSpec((None, 1, tn), lambda i, j: (row(i), 0, gate_chunk * nj + j))],
        out_specs=pl.BlockSpec((tm, tn), lambda i, j: (i, j)),
        out_shape=jax.ShapeDtypeStruct((t, D_MODEL), F32),
        compiler_params=_cp(("parallel", "parallel")),
        name="out_proj",
    )(a, w, *x_parts, mod3)


def _peer_candidate_blocks():
    kk = PEER_TOPK
    blocks = [(a, a + 1, kk // (a + 1)) for a in range(kk // 2)]
    blocks.append((kk // 2, kk, 1))
    return blocks


def _peer_route_kernel(h2t_ref, wqt_ref, keys_ref, s1_ref, e1_ref, s2_ref, e2_ref, qt_s, sv_s, cand_s):
    tt = h2t_ref.shape[1]
    kk = PEER_TOPK
    lanes = 128
    qt_s[...] = jnp.dot(wqt_ref[...], h2t_ref[...], preferred_element_type=F32).astype(BF16)

    def score_step(h, carry):
        for half, s_ref in ((0, s1_ref), (1, s2_ref)):
            c = 2 * h + half
            r0 = pl.multiple_of(c * PEER_HALF, PEER_HALF)
            s_ref[h] = jnp.dot(keys_ref[c], qt_s[pl.ds(r0, PEER_HALF), :], preferred_element_type=F32)
            for lt in range(tt // lanes):
                ln = slice(lt * lanes, (lt + 1) * lanes)
                cur = s_ref[h, :, ln]
                for r in range(kk):
                    m = jnp.max(cur, axis=0, keepdims=True)
                    sv_s[c, r:r + 1, ln] = m
                    cur = jnp.where(cur == m, -jnp.inf, cur)
        return carry

    lax.fori_loop(0, PEER_HEADS, score_step, 0)

    blocks = _peer_candidate_blocks()
    sub = lax.broadcasted_iota(jnp.int32, (8, lanes), 0)

    def head_step(h, carry):
        for lt in range(tt // lanes):
            ln = slice(lt * lanes, (lt + 1) * lanes)
            sv1 = sv_s[2 * h, :, ln]
            sv2 = sv_s[2 * h + 1, :, ln]
            row = 0
            for a_lo, a_hi, n_b in blocks:
                if a_hi - a_lo == 1:
                    n_rows = -(-n_b // 8) * 8
                    blk = sv1[a_lo:a_lo + 1, :] + sv2[0:n_rows, :]
                    if n_b < n_rows:
                        blk = jnp.where(sub < n_b, blk, -jnp.inf)
                else:
                    n_rows = a_hi - a_lo
                    blk = sv1[a_lo:a_hi, :] + sv2[0:1, :]
                cand_s[row:row + n_rows, ln] = blk
                row += n_rows
            cur = cand_s[:, ln]
            m0 = sv1[0:1, :] + sv2[0:1, :]
            taken = jnp.zeros((1, lanes), F32)
            tau = m0
            zsum = jnp.zeros((1, lanes), F32)
            for r in range(kk):
                m = jnp.max(cur, axis=0, keepdims=True)
                eq = cur == m
                cnt = jnp.sum(jnp.where(eq, 1.0, 0.0), axis=0, keepdims=True)
                active = taken < kk
                tau = jnp.where(active, m, tau)
                zsum = zsum + jnp.where(active, cnt * jnp.exp(m - m0), 0.0)
                taken = taken + jnp.where(active, cnt, 0.0)
                cur = jnp.where(eq, -jnp.inf, cur)
            s1 = s1_ref[h, :, ln]
            th1 = jnp.full(s1.shape, jnp.inf, F32)
            for b in range(kk):
                th1 = jnp.where(s1 + sv2[b:b + 1, :] >= tau, sv2[b:b + 1, :], th1)
            s1_ref[h, :, ln] = th1
            e1_ref[h, :, ln] = jnp.exp(s1 - sv1[0:1, :]) / zsum
            e2_ref[h, :, ln] = jnp.exp(s2_ref[h, :, ln] - sv2[0:1, :])
        return carry

    lax.fori_loop(0, PEER_HEADS, head_step, 0)


def _peer_route(h2t, wqt, keys):
    t = h2t.shape[1]
    tt = 512
    n_cand_rows = sum((-(-n_b // 8) * 8) if a_hi - a_lo == 1 else a_hi - a_lo
                      for a_lo, a_hi, n_b in _peer_candidate_blocks())
    slab = pl.BlockSpec((PEER_HEADS, PEER_N_KEYS, tt), lambda i: (0, 0, i))
    slab_shape = jax.ShapeDtypeStruct((PEER_HEADS, PEER_N_KEYS, t), F32)
    return pl.pallas_call(
        _peer_route_kernel,
        grid=(t // tt,),
        in_specs=[pl.BlockSpec((D_MODEL, tt), lambda i: (0, i)),
                  pl.BlockSpec(wqt.shape, lambda i: (0, 0)),
                  pl.BlockSpec(keys.shape, lambda i: (0, 0, 0))],
        out_specs=[slab, slab, slab, slab],
        out_shape=[slab_shape, slab_shape, slab_shape, slab_shape],
        scratch_shapes=[pltpu.VMEM((wqt.shape[0], tt), BF16),
                        pltpu.VMEM((2 * PEER_HEADS, PEER_TOPK, tt), F32),
                        pltpu.VMEM((n_cand_rows, tt), F32)],
        compiler_params=_cp(("parallel",)),
        name="peer_route",
    )(h2t, wqt, keys)


def _gelu_exact(x):
    return 0.5 * x * (1.0 + lax.erf(x * (2.0 ** -0.5)))


PEER_ROWS_PER_TILE = 8
PEER_J_CHUNK = 32
PEER_ROW_GROUP = 8
PEER_ACT_SLAB = 32
PEER_K_SLICE = 256


def _peer_dense_kernel(h2t_ref, u_ref, vt_ref, th1_ref, e1_ref, s2_ref, e2_ref, o_ref, g_s, pre_s, act_s):
    @pl.when(pl.program_id(1) == 0)
    def _():
        o_ref[...] = jnp.zeros_like(o_ref)

    tt = h2t_ref.shape[1]
    jc = PEER_J_CHUNK
    n_k = h2t_ref.shape[0] // PEER_K_SLICE
    chunks_per_k = (tt // 128) * (PEER_N_KEYS // jc) // n_k
    pace = []
    chunk = 0
    for lt in range(tt // 128):
        ln = slice(lt * 128, (lt + 1) * 128)
        for j0 in range(0, PEER_N_KEYS, jc):
            for i0 in range(0, PEER_ROWS_PER_TILE, PEER_ROW_GROUP):
                rows = range(i0, i0 + PEER_ROW_GROUP)
                acc = {ii: None for ii in rows}
                for h in range(PEER_HEADS):
                    s2 = s2_ref[h, j0:j0 + jc, ln]
                    e2 = e2_ref[h, j0:j0 + jc, ln]
                    for ii in rows:
                        term = jnp.where(s2 >= th1_ref[h, ii:ii + 1, ln], e2, 0.0) * e1_ref[h, ii:ii + 1, ln]
                        acc[ii] = term if acc[ii] is None else acc[ii] + term
                for ii in rows:
                    sl = slice(ii * PEER_N_KEYS + j0, ii * PEER_N_KEYS + j0 + jc)
                    g_s[sl, ln] = acc[ii]
            if chunk % chunks_per_k == 0:
                whole = functools.reduce(jnp.add, [acc[ii] for ii in rows])
                pace.append(jnp.minimum(jnp.sum(whole, axis=0, keepdims=True), 0.0))
            chunk += 1

    pre = None
    for k in range(n_k):
        ks = slice(k * PEER_K_SLICE, (k + 1) * PEER_K_SLICE)
        zero = jnp.concatenate([pace[k]] * (tt // 128), axis=1).astype(BF16)
        part = jnp.dot(u_ref[:, ks], h2t_ref[ks, :] + zero, preferred_element_type=F32)
        pre = part if pre is None else pre + part
    pre_s[...] = pre
    for r0 in range(0, g_s.shape[0], PEER_ACT_SLAB):
        sl = slice(r0, r0 + PEER_ACT_SLAB)
        act_s[sl, :] = (g_s[sl, :] * _gelu_exact(pre_s[sl, :])).astype(BF16)
    o_ref[...] += jnp.dot(vt_ref[...], act_s[...], preferred_element_type=F32)


def _peer_dense(h2t, u, vt, th1, e1, s2, e2):
    t = h2t.shape[1]
    n_exp = u.shape[0]
    tt = 512
    te = PEER_ROWS_PER_TILE * PEER_N_KEYS
    rows = pl.BlockSpec((PEER_HEADS, PEER_ROWS_PER_TILE, tt), lambda i, j: (0, j, i))
    slab = pl.BlockSpec((PEER_HEADS, PEER_N_KEYS, tt), lambda i, j: (0, 0, i))
    return pl.pallas_call(
        _peer_dense_kernel,
        grid=(t // tt, n_exp // te),
        in_specs=[pl.BlockSpec((D_MODEL, tt), lambda i, j: (0, i)),
                  pl.BlockSpec((te, D_MODEL), lambda i, j: (j, 0)),
                  pl.BlockSpec((D_MODEL, te), lambda i, j: (0, j)),
                  rows, rows, slab, slab],
        out_specs=pl.BlockSpec((D_MODEL, tt), lambda i, j: (0, i)),
        out_shape=jax.ShapeDtypeStruct((D_MODEL, t), F32),
        scratch_shapes=[pltpu.VMEM((te, tt), F32), pltpu.VMEM((te, tt), F32), pltpu.VMEM((te, tt), BF16)],
        compiler_params=_cp(("parallel", "arbitrary"), vmem_mb=56),
        name="peer_dense",
    )(h2t, u, vt, th1, e1, s2, e2)


def _final_kernel(x_ref, pt_ref, g2_ref, fg_ref, oc_ref, ol_ref, *, n_ctx_tiles):
    x = x_ref[...] + g2_ref[...] * pt_ref[...].T
    y = x * lax.rsqrt(jnp.mean(x * x, axis=-1, keepdims=True) + RMS_EPS) * fg_ref[...]
    ol_ref[...] = y

    @pl.when(pl.program_id(0) < n_ctx_tiles)
    def _():
        oc_ref[...] = y


def _final(x, p_t, mod3, gate_chunk, final_g, n_ctx_rows):
    t = x.shape[0]
    tm = 256
    nct = n_ctx_rows // tm
    row = functools.partial(_mod_row, tm=tm, n_ctx_rows=n_ctx_rows)
    return pl.pallas_call(
        functools.partial(_final_kernel, n_ctx_tiles=nct),
        grid=(t // tm,),
        in_specs=[pl.BlockSpec((tm, D_MODEL), lambda i: (i, 0)),
                  pl.BlockSpec((D_MODEL, tm), lambda i: (0, i)),
                  pl.BlockSpec((None, 1, D_MODEL), lambda i: (row(i), 0, gate_chunk)),
                  pl.BlockSpec((1, D_MODEL), lambda i: (0, 0))],
        out_specs=[pl.BlockSpec((tm, D_MODEL), lambda i: (_ctx_tile(i, nct), 0)),
                   pl.BlockSpec((tm, D_MODEL), lambda i: (_lat_tile(i, nct), 0))],
        out_shape=[jax.ShapeDtypeStruct((n_ctx_rows, D_MODEL), F32),
                   jax.ShapeDtypeStruct((t - n_ctx_rows, D_MODEL), F32)],
        compiler_params=_cp(("arbitrary",)),
        name="final_norm",
    )(x, p_t, mod3, final_g)


def _group_heads(p):
    return jnp.transpose(p.reshape(2, SSD_GROUPS, HEADS_PER_GROUP), (1, 0, 2)).reshape(
        SSD_GROUPS, 2 * HEADS_PER_GROUP)


def _layer(x_parts, lp, mod3, n_ctx_batch, n_lat_batch, cache):
    n_ctx_rows = n_ctx_batch * SEQ
    t = n_ctx_rows + n_lat_batch * DEC_SEQ
    w_in = lp['w_in']
    na_w = NA_HEADS * NA_HEAD_DIM
    o_z = 3 * na_w
    o_xbc = o_z + SSD_D_INNER
    o_dt = o_xbc + SSD_CONV_CH
    o_g = o_dt + 2 * SSD_HEADS

    h1 = _norm_mod(x_parts, lp['norm1_g'][None, :], mod3, 1, 0, n_ctx_rows)
    q = _mm(h1, w_in, 0, na_w, BF16, name="in_q")
    k_new, k_lat = _mm_split(h1, w_in, na_w, na_w, F32, n_ctx_rows, name="in_k")
    v_new, v_lat = _mm_split(h1, w_in, 2 * na_w, na_w, F32, n_ctx_rows, name="in_v")
    z = _mm(h1, w_in, o_z, SSD_D_INNER, F32, name="in_z")
    xbc = _mm(h1, w_in, o_xbc, SSD_CONV_CH, F32, name="in_xbc")
    dt_raw = _mm(h1, w_in, o_dt, 2 * SSD_HEADS, F32, name="in_dt")
    gates = _mm(h1, w_in[:, o_g:], 0, 2 * D_MODEL, F32, name="in_gates")

    k_ctx, v_ctx, init_f, init_b = cache
    na_ctx = _ctx_attention(q, k_new, v_new, n_ctx_batch)
    na_lat = _na_latent(q, k_lat, v_lat, k_ctx.reshape(n_lat_batch, -1, na_w),
                        v_ctx.reshape(n_lat_batch, -1, na_w), lp['na_rpb'], n_ctx_rows, n_lat_batch)

    dtg = jnp.transpose(dt_raw.reshape(t, 2, SSD_GROUPS, HEADS_PER_GROUP), (2, 0, 1, 3)).reshape(
        SSD_GROUPS, t, 2 * HEADS_PER_GROUP)
    dtgt = jnp.transpose(dtg, (0, 2, 1))
    conv_w8 = jnp.concatenate([lp['conv_w'], jnp.zeros((8 - SSD_CONV, SSD_CONV_CH), F32)], axis=0)
    dtb = _group_heads(lp['ssd_dt_bias'])
    alog = _group_heads(lp['ssd_a_log'])
    dexp = jnp.repeat(lp['ssd_d'], SSD_HEAD_DIM)[None, :]
    ssd_args = (xbc, z, dtg, dtgt, conv_w8, lp['conv_b'][None, :], dtb[:, None, :], dtb[:, :, None],
                alog[:, None, :], alog[:, :, None], dexp, lp['ssd_norm_g'][None, :])
    ssd_ctx, s_f, s_b = _ssd(*ssd_args, row_start=0, nb=n_ctx_batch, seq=SEQ)
    ssd_lat = _ssd(*ssd_args, row_start=n_ctx_rows, nb=n_lat_batch, seq=DEC_SEQ, init=(init_f, init_b))

    mixed = _mix((na_ctx, na_lat), (ssd_ctx, ssd_lat), lp['w_na_proj'].astype(BF16),
                 lp['w_ssd_proj'].astype(BF16), gates, n_ctx_rows)
    x1 = _outproj_residual(mixed, lp['w_out'].astype(BF16), x_parts, mod3, 2, n_ctx_rows)

    h2t = _norm_mod((x1,), lp['norm2_g'][None, :], mod3, 4, 3, n_ctx_rows, transpose_out=True)
    keys = lp['peer_keys'].reshape(2 * PEER_HEADS, PEER_N_KEYS, PEER_HALF).astype(BF16)
    th1, e1, s2, e2 = _peer_route(h2t, lp['peer_wq'].T.astype(BF16), keys)
    peer_t = _peer_dense(h2t, lp['peer_u'].astype(BF16), lp['peer_v'].T.astype(BF16), th1, e1, s2, e2)
    new_k = k_new.reshape(n_ctx_batch, SEQ, NA_HEADS, NA_HEAD_DIM)
    new_v = v_new.reshape(n_ctx_batch, SEQ, NA_HEADS, NA_HEAD_DIM)
    return x1, peer_t, (new_k, new_v, s_f, s_b)


def kernel(x_prompt, x_sample, c, c_ctx, cache_na_k, cache_na_v, state_ssd_fwd, state_ssd_bwd, ada_w, ada_b, norm1_g, norm2_g, w_in, conv_w, conv_b, na_rpb, ssd_a_log, ssd_dt_bias, ssd_d, ssd_norm_g, w_na_proj, w_ssd_proj, w_out, peer_wq, peer_keys, peer_u, peer_v, final_g):
    n_ctx_batch = x_prompt.shape[0]
    n_lat_batch = x_sample.shape[0]
    depth = ada_w.shape[0]
    n_ctx_rows = n_ctx_batch * SEQ
    assert x_prompt.shape[1] == SEQ and x_sample.shape[1] == DEC_SEQ
    assert 1 + n_lat_batch <= MOD_ROWS and n_ctx_rows % DEC_SEQ == 0
    assert depth == 1, "the final-norm kernel folds in the PEER residual of the single trunk layer"

    x_parts = (x_prompt.reshape(-1, D_MODEL), x_sample.reshape(-1, D_MODEL))
    cvec = jnp.concatenate([c_ctx[None, :], c, jnp.zeros((MOD_ROWS - 1 - n_lat_batch, D_MODEL), F32)], axis=0)
    lp = {'norm1_g': norm1_g[0], 'norm2_g': norm2_g[0], 'w_in': w_in[0], 'conv_w': conv_w[0],
          'conv_b': conv_b[0], 'na_rpb': na_rpb[0], 'ssd_a_log': ssd_a_log[0],
          'ssd_dt_bias': ssd_dt_bias[0], 'ssd_d': ssd_d[0], 'ssd_norm_g': ssd_norm_g[0],
          'w_na_proj': w_na_proj[0], 'w_ssd_proj': w_ssd_proj[0], 'w_out': w_out[0],
          'peer_wq': peer_wq[0], 'peer_keys': peer_keys[0], 'peer_u': peer_u[0], 'peer_v': peer_v[0]}
    mod3 = _ada(cvec, ada_w[0], ada_b[0][None, :])[:, None, :]
    cache = (cache_na_k[:, 0], cache_na_v[:, 0], state_ssd_fwd[:, 0], state_ssd_bwd[:, 0])
    x1, peer_t, (new_k, new_v, new_f, new_b) = _layer(x_parts, lp, mod3, n_ctx_batch, n_lat_batch, cache)
    y_ctx, y_lat = _final(x1, peer_t, mod3, 5, final_g[None, :], n_ctx_rows)
    return (y_ctx.reshape(x_prompt.shape), y_lat.reshape(x_sample.shape),
            new_k[:, None], new_v[:, None], new_f[:, None], new_b[:, None])
```

```python
import functools

import jax
import jax.numpy as jnp
from jax import lax
from jax.experimental import pallas as pl
from jax.experimental.pallas import tpu as pltpu

F32 = jnp.float32
BF16 = jnp.bfloat16

D_MODEL = 2048
SEQ = 256
DEC_SEQ = 1024
GRID_W = 64
NA_HEADS = 16
NA_HEAD_DIM = 128
NA_WIN_ROWS = 8
NA_WIN_COLS = 16
SSD_D_INNER = 4096
SSD_HEAD_DIM = 64
SSD_HEADS = 64
SSD_GROUPS = 8
SSD_STATE = 128
SSD_CONV = 5
SSD_CHUNK = 128
SSD_CONV_CH = SSD_D_INNER + 2 * SSD_GROUPS * SSD_STATE
HEADS_PER_GROUP = SSD_HEADS // SSD_GROUPS
GROUP_CH = SSD_D_INNER // SSD_GROUPS
PEER_HEADS = 8
PEER_HALF = 128
PEER_N_KEYS = 128
PEER_TOPK = 16
RMS_EPS = 1e-6
NEG_BIG = -1e30
MOD_ROWS = 8
VMEM_LIMIT_MB = 48

NT_DIMS = (((1,), (1,)), ((), ()))


def _cp(sem, vmem_mb=VMEM_LIMIT_MB, flags=None):
    return pltpu.CompilerParams(dimension_semantics=sem, vmem_limit_bytes=vmem_mb * 1024 * 1024, flags=flags)


def _mod_row(i, tm, n_ctx_rows):
    n_ctx_tiles = n_ctx_rows // tm
    return jnp.where(i < n_ctx_tiles, 0, 1 + (i - n_ctx_tiles) // (DEC_SEQ // tm))


def _silu(x):
    return x * jax.nn.sigmoid(x)


def _softplus(x):
    return jnp.maximum(x, 0.0) + jnp.log1p(jnp.exp(-jnp.abs(x)))


def _split3(x):
    hi = x.astype(BF16)
    r1 = x - hi.astype(F32)
    mid = r1.astype(BF16)
    lo = (r1 - mid.astype(F32)).astype(BF16)
    return hi, mid, lo


def _dot_f32_by_01(x, m01):
    return sum(jnp.dot(p, m01, preferred_element_type=F32) for p in _split3(x))


def _dot_01_by_f32(m01, x):
    return sum(jnp.dot(m01, p, preferred_element_type=F32) for p in _split3(x))


def _ada_kernel(c_ref, w_ref, b_ref, o_ref):
    s = _silu(c_ref[...])
    o_ref[...] = jnp.dot(s, w_ref[...], preferred_element_type=F32,
                         precision=lax.Precision.HIGHEST) + b_ref[...]


def _ada(cvec, ada_w, ada_b):
    n = ada_w.shape[1]
    tn = 1024
    return pl.pallas_call(
        _ada_kernel,
        grid=(n // tn,),
        in_specs=[pl.BlockSpec((MOD_ROWS, D_MODEL), lambda j: (0, 0)),
                  pl.BlockSpec((D_MODEL, tn), lambda j: (0, j)),
                  pl.BlockSpec((1, tn), lambda j: (0, j))],
        out_specs=pl.BlockSpec((MOD_ROWS, tn), lambda j: (0, j)),
        out_shape=jax.ShapeDtypeStruct((MOD_ROWS, n), F32),
        compiler_params=_cp(("parallel",)),
        name="ada_mod",
    )(cvec, ada_w, ada_b)


def _norm_mod_kernel(xc_ref, xl_ref, g_ref, sc_ref, sh_ref, o_ref, *, n_ctx_tiles):
    x = jnp.where(pl.program_id(0) < n_ctx_tiles, xc_ref[...], xl_ref[...])
    y = x * lax.rsqrt(jnp.mean(x * x, axis=-1, keepdims=True) + RMS_EPS) * g_ref[...]
    o_ref[...] = (y * (1.0 + sc_ref[...]) + sh_ref[...]).astype(o_ref.dtype)


def _norm_mod(x_parts, g, mod3, sc_chunk, sh_chunk, n_ctx_rows):
    t = sum(p.shape[0] for p in x_parts)
    tm = 512
    nct = n_ctx_rows // tm
    row = functools.partial(_mod_row, tm=tm, n_ctx_rows=n_ctx_rows)
    return pl.pallas_call(
        functools.partial(_norm_mod_kernel, n_ctx_tiles=nct),
        grid=(t // tm,),
        in_specs=[pl.BlockSpec((tm, D_MODEL), lambda i: (_ctx_tile(i, nct), 0)),
                  pl.BlockSpec((tm, D_MODEL), lambda i: (_lat_tile(i, nct), 0)),
                  pl.BlockSpec((1, D_MODEL), lambda i: (0, 0)),
                  pl.BlockSpec((None, 1, D_MODEL), lambda i: (row(i), 0, sc_chunk)),
                  pl.BlockSpec((None, 1, D_MODEL), lambda i: (row(i), 0, sh_chunk))],
        out_specs=pl.BlockSpec((tm, D_MODEL), lambda i: (i, 0)),
        out_shape=jax.ShapeDtypeStruct((t, D_MODEL), BF16),
        compiler_params=_cp(("parallel",)),
        name="norm_mod",
    )(*x_parts, g, mod3, mod3)


def _mm_kernel(a_ref, w_ref, o_ref, wb_s):
    @pl.when(pl.program_id(1) == 0)
    def _():
        wb_s[...] = w_ref[...].astype(BF16)

    o_ref[...] = jnp.dot(a_ref[...], wb_s[...], preferred_element_type=F32).astype(o_ref.dtype)


def _mm(a, w, col0, n, out_dtype, tm=1024, tn=1024, name="mm"):
    m, k = a.shape
    tn = min(tn, n)
    assert col0 % tn == 0 and n % tn == 0 and m % tm == 0
    c0 = col0 // tn
    return pl.pallas_call(
        _mm_kernel,
        grid=(n // tn, m // tm),
        in_specs=[pl.BlockSpec((tm, k), lambda j, i: (i, 0)),
                  pl.BlockSpec((k, tn), lambda j, i: (0, c0 + j))],
        out_specs=pl.BlockSpec((tm, tn), lambda j, i: (i, j)),
        out_shape=jax.ShapeDtypeStruct((m, n), out_dtype),
        scratch_shapes=[pltpu.VMEM((k, tn), BF16)],
        compiler_params=_cp(("arbitrary", "arbitrary"), vmem_mb=56),
        name=name,
    )(a, w)


def _ctx_tile(i, n_ctx_tiles):
    return jnp.minimum(i, n_ctx_tiles - 1)


def _lat_tile(i, n_ctx_tiles):
    return jnp.maximum(i - n_ctx_tiles, 0)


def _mm_split_kernel(a_ref, w_ref, oc_ref, ol_ref, wb_s, *, n_ctx_tiles):
    @pl.when(pl.program_id(1) == 0)
    def _():
        wb_s[...] = w_ref[...].astype(BF16)

    y = jnp.dot(a_ref[...], wb_s[...], preferred_element_type=F32)
    ol_ref[...] = y.astype(ol_ref.dtype)

    @pl.when(pl.program_id(1) < n_ctx_tiles)
    def _():
        oc_ref[...] = y.astype(oc_ref.dtype)


def _mm_split(a, w, col0, n, out_dtype, n_ctx_rows, tm=1024, tn=1024, name="mm_split"):
    m, k = a.shape
    assert col0 % tn == 0 and n % tn == 0 and m % tm == 0 and n_ctx_rows % tm == 0
    nct = n_ctx_rows // tm
    c0 = col0 // tn
    return pl.pallas_call(
        functools.partial(_mm_split_kernel, n_ctx_tiles=nct),
        grid=(n // tn, m // tm),
        in_specs=[pl.BlockSpec((tm, k), lambda j, i: (i, 0)),
                  pl.BlockSpec((k, tn), lambda j, i: (0, c0 + j))],
        out_specs=[pl.BlockSpec((tm, tn), lambda j, i: (_ctx_tile(i, nct), j)),
                   pl.BlockSpec((tm, tn), lambda j, i: (_lat_tile(i, nct), j))],
        out_shape=[jax.ShapeDtypeStruct((n_ctx_rows, n), out_dtype),
                   jax.ShapeDtypeStruct((m - n_ctx_rows, n), out_dtype)],
        scratch_shapes=[pltpu.VMEM((k, tn), BF16)],
        compiler_params=_cp(("arbitrary", "arbitrary"), vmem_mb=56),
        name=name,
    )(a, w)


def _ctx_attn_kernel(q_ref, k_ref, v_ref, o_ref):
    scale = NA_HEAD_DIM ** -0.5
    for h in range(NA_HEADS):
        sl = slice(h * NA_HEAD_DIM, (h + 1) * NA_HEAD_DIM)
        q = q_ref[:, sl]
        k = k_ref[:, sl].astype(BF16)
        v = v_ref[:, sl].astype(BF16)
        s = lax.dot_general(q, k, NT_DIMS, preferred_element_type=F32) * scale
        p = jnp.exp(s - jnp.max(s, axis=-1, keepdims=True))
        l = jnp.sum(p, axis=-1, keepdims=True)
        o = jnp.dot(p.astype(BF16), v, preferred_element_type=F32) / l
        o_ref[:, sl] = o.astype(o_ref.dtype)


def _ctx_attention(q, k, v, n_ctx_batch):
    width = NA_HEADS * NA_HEAD_DIM
    spec = pl.BlockSpec((SEQ, width), lambda b: (b, 0))
    return pl.pallas_call(
        _ctx_attn_kernel,
        grid=(n_ctx_batch,),
        in_specs=[spec, spec, spec],
        out_specs=spec,
        out_shape=jax.ShapeDtypeStruct((n_ctx_batch * SEQ, width), BF16),
        compiler_params=_cp(("parallel",)),
        name="ctx_attn",
    )(q, k, v)


def _na_lat_kernel(q_ref, k_ref, v_ref, kc_ref, vc_ref, rpb_ref, o_ref, pair_s, sc_s, pc_s, ow_s):
    scale = NA_HEAD_DIM ** -0.5
    rows = DEC_SEQ // GRID_W
    kc = kc_ref[...].astype(BF16)
    vc = vc_ref[...].astype(BF16)

    two_w = 2 * GRID_W
    qi = lax.broadcasted_iota(jnp.int32, (GRID_W, two_w), 0)
    lane = lax.broadcasted_iota(jnp.int32, (GRID_W, two_w), 1)
    kj = lane & (GRID_W - 1)
    col_start = jnp.clip(qi - NA_WIN_COLS // 2, 0, GRID_W - NA_WIN_COLS)
    in_window = (kj >= col_start) & (kj < col_start + NA_WIN_COLS)
    first_half = lane < GRID_W
    n_off = 2 * NA_WIN_ROWS - 1

    def toeplitz(d, lane_off):
        row = jnp.broadcast_to(rpb_ref[d:d + 1, :], (GRID_W, two_w))
        return pltpu.roll(row, (lane_off - (NA_WIN_COLS - 1)) % two_w, 1, stride=1, stride_axis=0)

    for d in range(n_off - 1):
        both = jnp.where(first_half, toeplitz(d, 0), toeplitz(d + 1, GRID_W))
        pair_s[d] = jnp.where(in_window, both, NEG_BIG)

    sc_s[...] = lax.dot_general(q_ref[...], kc, NT_DIMS, preferred_element_type=F32) * scale

    def row_start(r):
        return min(max(r - NA_WIN_ROWS // 2, 0), rows - NA_WIN_ROWS)

    groups = []
    for r in range(rows):
        if groups and row_start(groups[-1][0]) == row_start(r):
            groups[-1].append(r)
        else:
            groups.append([r])
    for grp in groups:
        rs = row_start(grp[0])
        qs = slice(grp[0] * GRID_W, (grp[-1] + 1) * GRID_W)
        win = slice(rs * GRID_W, (rs + NA_WIN_ROWS) * GRID_W)
        kw = k_ref[win, :].astype(BF16)
        vw = v_ref[win, :].astype(BF16)
        sw_all = lax.dot_general(q_ref[qs, :], kw, NT_DIMS, preferred_element_type=F32) * scale
        pws, inv_ls = [], []
        for n, r in enumerate(grp):
            rq = slice(r * GRID_W, (r + 1) * GRID_W)
            bias = jnp.concatenate([pair_s[rs + 2 * kp - r + NA_WIN_ROWS - 1] for kp in range(NA_WIN_ROWS // 2)],
                                   axis=1)
            sw = sw_all[n * GRID_W:(n + 1) * GRID_W, :] + bias
            sc = sc_s[rq, :]
            m = jnp.maximum(jnp.max(sw, axis=-1, keepdims=True), jnp.max(sc, axis=-1, keepdims=True))
            pw = jnp.exp(sw - m)
            pc = jnp.exp(sc - m)
            inv_l = 1.0 / (jnp.sum(pw, axis=-1, keepdims=True) + jnp.sum(pc, axis=-1, keepdims=True))
            pc_s[rq, :] = (pc * inv_l).astype(BF16)
            pws.append(pw.astype(BF16))
            inv_ls.append(inv_l)
        ow = jnp.dot(jnp.concatenate(pws, axis=0), vw, preferred_element_type=F32)
        ow_s[qs, :] = ow * jnp.concatenate(inv_ls, axis=0)
    o = ow_s[...] + jnp.dot(pc_s[...], vc, preferred_element_type=F32)
    o_ref[...] = o.astype(o_ref.dtype)


def _na_latent(q, k, v, k_ctx, v_ctx, rpb, n_ctx_rows, n_lat_batch):
    off = n_ctx_rows // DEC_SEQ
    n_off = 2 * NA_WIN_ROWS - 1
    rpb_pad = jnp.zeros((NA_HEADS, 16, 2 * GRID_W), F32).at[:, :n_off, :2 * NA_WIN_COLS - 1].set(rpb)
    q_tok = pl.BlockSpec((DEC_SEQ, NA_HEAD_DIM), lambda b, h: (off + b, h))
    tok = pl.BlockSpec((DEC_SEQ, NA_HEAD_DIM), lambda b, h: (b, h))
    ctx = pl.BlockSpec((None, k_ctx.shape[1], NA_HEAD_DIM), lambda b, h: (b, 0, h))
    return pl.pallas_call(
        _na_lat_kernel,
        grid=(n_lat_batch, NA_HEADS),
        in_specs=[q_tok, tok, tok, ctx, ctx,
                  pl.BlockSpec((None, 16, 2 * GRID_W), lambda b, h: (h, 0, 0))],
        out_specs=pl.BlockSpec((DEC_SEQ, NA_HEAD_DIM), lambda b, h: (b, h)),
        out_shape=jax.ShapeDtypeStruct((n_lat_batch * DEC_SEQ, NA_HEADS * NA_HEAD_DIM), BF16),
        scratch_shapes=[pltpu.VMEM((n_off - 1, GRID_W, 2 * GRID_W), F32),
                        pltpu.VMEM((DEC_SEQ, k_ctx.shape[1]), F32),
                        pltpu.VMEM((DEC_SEQ, k_ctx.shape[1]), BF16),
                        pltpu.VMEM((DEC_SEQ, NA_HEAD_DIM), F32)],
        compiler_params=_cp(("parallel", "parallel")),
        name="na_latent",
    )(q, k, v, k_ctx, v_ctx, rpb_pad)


def _ssd_kernel(*refs, seq, has_init):
    n_in = 20 if has_init else 18
    (xs_ref, b_ref, c_ref, z_ref, dt_ref, dtt_ref, cwx_ref, cwb_ref, cwc_ref,
     cbx_ref, cbb_ref, cbc_ref, dtb_ref, dtbt_ref, al_ref, alt_ref, dexp_ref, ng_ref) = refs[:18]
    if has_init:
        if_ref, ib_ref = refs[18:20]
        (y_ref,) = refs[n_in:n_in + 1]
        scratch = refs[n_in + 1:]
    else:
        y_ref, sfo_ref, sbo_ref = refs[n_in:n_in + 3]
        scratch = refs[n_in + 3:]
    pad_s, xs_s, bb_s, cc_s, dt_s, cumc_s, cumr_s, dtr_s, ddt_s, dtot_s, yacc_s, sf_s, sb_s = scratch

    nc = seq // SSD_CHUNK
    ck = SSD_CHUNK
    halo = 8

    def conv_silu(dst_ref, src_ref, w_ref, bias_ref, width):
        pad_s[0:halo, 0:width] = jnp.zeros((halo, width), F32)
        pad_s[seq + halo:seq + 2 * halo, 0:width] = jnp.zeros((halo, width), F32)
        pad_s[halo:seq + halo, 0:width] = src_ref[...]
        for ci in range(nc):
            base = halo - SSD_CONV // 2 + ci * ck
            acc = bias_ref[...] + pad_s[base:base + ck, 0:width] * w_ref[0:1, :]
            for tap in range(1, SSD_CONV):
                acc = acc + pad_s[base + tap:base + tap + ck, 0:width] * w_ref[tap:tap + 1, :]
            dst_ref[ci * ck:(ci + 1) * ck, :] = _silu(acc)

    conv_silu(xs_s, xs_ref, cwx_ref, cbx_ref, GROUP_CH)
    conv_silu(bb_s, b_ref, cwb_ref, cbb_ref, SSD_STATE)
    conv_silu(cc_s, c_ref, cwc_ref, cbc_ref, SSD_STATE)

    li = lax.broadcasted_iota(jnp.int32, (ck, ck), 0)
    si = lax.broadcasted_iota(jnp.int32, (ck, ck), 1)
    lower = li >= si
    upper = si >= li
    tri_lo = jnp.where(lower, 1.0, 0.0).astype(BF16)
    tri_up = jnp.where(upper, 1.0, 0.0).astype(BF16)
    nh2 = 2 * HEADS_PER_GROUP

    dt_s[...] = _softplus(dt_ref[...] + dtb_ref[...])
    a_row = -jnp.exp(al_ref[...])
    dtt = _softplus(dtt_ref[...] + dtbt_ref[...])
    dat = dtt * (-jnp.exp(alt_ref[...]))
    col_is_fwd = lax.broadcasted_iota(jnp.int32, (ck, nh2), 1) < HEADS_PER_GROUP
    row_is_fwd = lax.broadcasted_iota(jnp.int32, (nh2, ck), 0) < HEADS_PER_GROUP
    for c in range(nc):
        da = dt_s[c * ck:(c + 1) * ck, :] * a_row
        cumc_s[c] = jnp.where(col_is_fwd, _dot_01_by_f32(tri_lo, da), _dot_01_by_f32(tri_up, da))
        dat_c = dat[:, c * ck:(c + 1) * ck]
        cumr = jnp.where(row_is_fwd, _dot_f32_by_01(dat_c, tri_up), _dot_f32_by_01(dat_c, tri_lo))
        cumr_s[c] = cumr
        total = jnp.where(row_is_fwd, jnp.broadcast_to(cumr[:, ck - 1:ck], (nh2, ck)),
                          jnp.broadcast_to(cumr[:, 0:1], (nh2, ck)))
        dtt_c = dtt[:, c * ck:(c + 1) * ck]
        dtr_s[c] = dtt_c
        ddt_s[c] = jnp.exp(total - cumr) * dtt_c
        dtot_s[c] = jnp.exp(total)

    lane_lo = lax.broadcasted_iota(jnp.int32, (ck, 2 * SSD_HEAD_DIM), 1) < SSD_HEAD_DIM

    yacc_s[...] = jnp.zeros_like(yacc_s)
    if has_init:
        sf_s[...] = if_ref[...].reshape(GROUP_CH, SSD_STATE).T
        sb_s[...] = ib_ref[...].reshape(GROUP_CH, SSD_STATE).T
    else:
        sf_s[...] = jnp.zeros_like(sf_s)
        sb_s[...] = jnp.zeros_like(sb_s)

    def chunk_step(i, carry):
        for d in (0, 1):
            c = i if d == 0 else nc - 1 - i
            r0 = pl.multiple_of(c * ck, ck)
            x_bf = xs_s[pl.ds(r0, ck), :].astype(BF16)
            b_c = bb_s[pl.ds(r0, ck), :]
            c_f = cc_s[pl.ds(r0, ck), :]
            cumc = cumc_s[c]
            cumr = cumr_s[c]
            dtr = dtr_s[c]
            ddt = ddt_s[c]
            dtot = dtot_s[c]
            state_ref = sf_s if d == 0 else sb_s
            tri_mask = lower if d == 0 else upper

            cb = lax.dot_general(c_f.astype(BF16), b_c.astype(BF16), NT_DIMS, preferred_element_type=F32)
            bt = b_c.T
            state = state_ref[...]
            s_bf = state.astype(BF16)

            y_pieces, st_pieces = [], []
            for hp in range(HEADS_PER_GROUP // 2):
                cols = slice(hp * 2 * SSD_HEAD_DIM, (hp + 1) * 2 * SSD_HEAD_DIM)
                x_pair = x_bf[:, cols]
                rhs = jnp.concatenate([x_pair, s_bf[:, cols]], axis=0)
                ys, sts, dts = [], [], []
                for sub in range(2):
                    j = d * HEADS_PER_GROUP + hp * 2 + sub
                    cum_l = jnp.broadcast_to(cumc[:, j:j + 1], (ck, ck))
                    seg = cum_l - cumr[j:j + 1, :]
                    lmat = jnp.where(tri_mask, jnp.exp(seg), 0.0)
                    lhs = jnp.concatenate([cb * lmat * dtr[j:j + 1, :], c_f * jnp.exp(cum_l)], axis=1)
                    ys.append(jnp.dot(lhs.astype(BF16), rhs, preferred_element_type=F32))
                    w = (bt * ddt[j:j + 1, :]).astype(BF16)
                    sts.append(jnp.dot(w, x_pair, preferred_element_type=F32))
                    dts.append(dtot[j:j + 1, :])
                y_pieces.append(jnp.where(lane_lo, ys[0], ys[1]))
                decay = jnp.where(lane_lo[0:1, :], dts[0], dts[1])
                st_pieces.append(state[:, cols] * decay + jnp.where(lane_lo, sts[0], sts[1]))
            state_ref[...] = jnp.concatenate(st_pieces, axis=1)
            yacc_s[pl.ds(r0, ck), :] = yacc_s[pl.ds(r0, ck), :] + jnp.concatenate(y_pieces, axis=1)
        return carry

    lax.fori_loop(0, nc, chunk_step, 0)

    for ci in range(nc):
        rows = slice(ci * ck, (ci + 1) * ck)
        zz = z_ref[rows, :]
        y = (yacc_s[rows, :] + dexp_ref[...] * xs_s[rows, :]) * _silu(zz)
        y = y * lax.rsqrt(jnp.mean(y * y, axis=-1, keepdims=True) + RMS_EPS) * ng_ref[...]
        y_ref[rows, :] = y.astype(y_ref.dtype)

    if not has_init:
        sfo_ref[...] = sf_s[...].T.reshape(HEADS_PER_GROUP, SSD_HEAD_DIM, SSD_STATE)
        sbo_ref[...] = sb_s[...].T.reshape(HEADS_PER_GROUP, SSD_HEAD_DIM, SSD_STATE)


def _ssd(xbc, z, dtg, dtgt, conv_w8, conv_b, dtb, dtbt, alog, alogt, dexp, norm_g,
         row_start, nb, seq, init=None):
    off = row_start // seq
    g_b = SSD_D_INNER // SSD_STATE
    g_c = g_b + SSD_GROUPS
    has_init = init is not None
    in_specs = [
        pl.BlockSpec((seq, GROUP_CH), lambda b, g: (off + b, g)),
        pl.BlockSpec((seq, SSD_STATE), lambda b, g: (off + b, g_b + g)),
        pl.BlockSpec((seq, SSD_STATE), lambda b, g: (off + b, g_c + g)),
        pl.BlockSpec((seq, GROUP_CH), lambda b, g: (off + b, g)),
        pl.BlockSpec((None, seq, 2 * HEADS_PER_GROUP), lambda b, g: (g, off + b, 0)),
        pl.BlockSpec((None, 2 * HEADS_PER_GROUP, seq), lambda b, g: (g, 0, off + b)),
        pl.BlockSpec((8, GROUP_CH), lambda b, g: (0, g)),
        pl.BlockSpec((8, SSD_STATE), lambda b, g: (0, g_b + g)),
        pl.BlockSpec((8, SSD_STATE), lambda b, g: (0, g_c + g)),
        pl.BlockSpec((1, GROUP_CH), lambda b, g: (0, g)),
        pl.BlockSpec((1, SSD_STATE), lambda b, g: (0, g_b + g)),
        pl.BlockSpec((1, SSD_STATE), lambda b, g: (0, g_c + g)),
        pl.BlockSpec((None, 1, 2 * HEADS_PER_GROUP), lambda b, g: (g, 0, 0)),
        pl.BlockSpec((None, 2 * HEADS_PER_GROUP, 1), lambda b, g: (g, 0, 0)),
        pl.BlockSpec((None, 1, 2 * HEADS_PER_GROUP), lambda b, g: (g, 0, 0)),
        pl.BlockSpec((None, 2 * HEADS_PER_GROUP, 1), lambda b, g: (g, 0, 0)),
        pl.BlockSpec((1, GROUP_CH), lambda b, g: (0, g)),
        pl.BlockSpec((1, GROUP_CH), lambda b, g: (0, g)),
    ]
    args = [xbc, xbc, xbc, z, dtg, dtgt, conv_w8, conv_w8, conv_w8, conv_b, conv_b, conv_b,
            dtb, dtbt, alog, alogt, dexp, norm_g]
    state_spec = pl.BlockSpec((None, HEADS_PER_GROUP, SSD_HEAD_DIM, SSD_STATE), lambda b, g: (b, g, 0, 0))
    y_spec = pl.BlockSpec((seq, GROUP_CH), lambda b, g: (b, g))
    y_shape = jax.ShapeDtypeStruct((nb * seq, SSD_D_INNER), BF16)
    if has_init:
        in_specs += [state_spec, state_spec]
        args += list(init)
        out_specs = y_spec
        out_shape = y_shape
    else:
        st_shape = jax.ShapeDtypeStruct((nb, SSD_HEADS, SSD_HEAD_DIM, SSD_STATE), F32)
        out_specs = [y_spec, state_spec, state_spec]
        out_shape = [y_shape, st_shape, st_shape]
    nc = seq // SSD_CHUNK
    scratch = [
        pltpu.VMEM((seq + 16, GROUP_CH), F32),
        pltpu.VMEM((seq, GROUP_CH), F32),
        pltpu.VMEM((seq, SSD_STATE), F32),
        pltpu.VMEM((seq, SSD_STATE), F32),
        pltpu.VMEM((seq, 2 * HEADS_PER_GROUP), F32),
        pltpu.VMEM((nc, SSD_CHUNK, 2 * HEADS_PER_GROUP), F32),
        pltpu.VMEM((nc, 2 * HEADS_PER_GROUP, SSD_CHUNK), F32),
        pltpu.VMEM((nc, 2 * HEADS_PER_GROUP, SSD_CHUNK), F32),
        pltpu.VMEM((nc, 2 * HEADS_PER_GROUP, SSD_CHUNK), F32),
        pltpu.VMEM((nc, 2 * HEADS_PER_GROUP, SSD_CHUNK), F32),
        pltpu.VMEM((seq, GROUP_CH), F32),
        pltpu.VMEM((SSD_STATE, GROUP_CH), F32),
        pltpu.VMEM((SSD_STATE, GROUP_CH), F32),
    ]
    return pl.pallas_call(
        functools.partial(_ssd_kernel, seq=seq, has_init=has_init),
        grid=(nb, SSD_GROUPS),
        in_specs=in_specs,
        out_specs=out_specs,
        out_shape=out_shape,
        scratch_shapes=scratch,
        compiler_params=_cp(("parallel", "parallel")),
        name="ssd_lat" if has_init else "ssd_ctx",
    )(*args)


def _mix_kernel(nac_ref, nal_ref, ssdc_ref, ssdl_ref, wna_ref, wssd_ref, gna_ref, gssd_ref, o_ref, *, n_ctx_tiles):
    is_ctx = pl.program_id(0) < n_ctx_tiles
    na = jnp.where(is_ctx, nac_ref[...], nal_ref[...])
    ssd = jnp.where(is_ctx, ssdc_ref[...], ssdl_ref[...])
    a = jnp.dot(na, wna_ref[...], preferred_element_type=F32)
    s = jnp.dot(ssd, wssd_ref[...], preferred_element_type=F32)
    o = jax.nn.sigmoid(gna_ref[...]) * a + jax.nn.sigmoid(gssd_ref[...]) * s
    o_ref[...] = o.astype(o_ref.dtype)


def _mix(na_parts, ssd_parts, w_na, w_ssd, gates, n_ctx_rows):
    t = gates.shape[0]
    tm, tn = 512, 512
    nj = D_MODEL // tn
    nct = n_ctx_rows // tm
    ctx_rows = lambda i, j: (_ctx_tile(i, nct), 0)
    lat_rows = lambda i, j: (_lat_tile(i, nct), 0)
    return pl.pallas_call(
        functools.partial(_mix_kernel, n_ctx_tiles=nct),
        grid=(t // tm, nj),
        in_specs=[pl.BlockSpec((tm, w_na.shape[0]), ctx_rows),
                  pl.BlockSpec((tm, w_na.shape[0]), lat_rows),
                  pl.BlockSpec((tm, w_ssd.shape[0]), ctx_rows),
                  pl.BlockSpec((tm, w_ssd.shape[0]), lat_rows),
                  pl.BlockSpec((w_na.shape[0], tn), lambda i, j: (0, j)),
                  pl.BlockSpec((w_ssd.shape[0], tn), lambda i, j: (0, j)),
                  pl.BlockSpec((tm, tn), lambda i, j: (i, j)),
                  pl.BlockSpec((tm, tn), lambda i, j: (i, nj + j))],
        out_specs=pl.BlockSpec((tm, tn), lambda i, j: (i, j)),
        out_shape=jax.ShapeDtypeStruct((t, D_MODEL), BF16),
        compiler_params=_cp(("parallel", "parallel"), vmem_mb=56),
        name="branch_mix",
    )(*na_parts, *ssd_parts, w_na, w_ssd, gates, gates)


def _outproj_kernel(a_ref, w_ref, xc_ref, xl_ref, g1_ref, ng_ref, sc_ref, sh_ref, x1_ref, h2t_ref, *, n_ctx_tiles):
    y = jnp.dot(a_ref[...], w_ref[...], preferred_element_type=F32)
    x = jnp.where(pl.program_id(0) < n_ctx_tiles, xc_ref[...], xl_ref[...])
    x1 = x + g1_ref[...] * y
    x1_ref[...] = x1
    h = x1 * lax.rsqrt(jnp.mean(x1 * x1, axis=-1, keepdims=True) + RMS_EPS) * ng_ref[...]
    h = h * (1.0 + sc_ref[...]) + sh_ref[...]
    h2t_ref[...] = h.T.astype(h2t_ref.dtype)


def _outproj_residual_norm(a, w, x_parts, mod3, gate_chunk, norm_g, sc_chunk, sh_chunk, n_ctx_rows):
    t = a.shape[0]
    tm = 512
    nct = n_ctx_rows // tm
    row = functools.partial(_mod_row, tm=tm, n_ctx_rows=n_ctx_rows)
    mod_row = lambda chunk: pl.BlockSpec((None, 1, D_MODEL), lambda i: (row(i), 0, chunk))
    return pl.pallas_call(
        functools.partial(_outproj_kernel, n_ctx_tiles=nct),
        grid=(t // tm,),
        in_specs=[pl.BlockSpec((tm, a.shape[1]), lambda i: (i, 0)),
                  pl.BlockSpec(w.shape, lambda i: (0, 0)),
                  pl.BlockSpec((tm, D_MODEL), lambda i: (_ctx_tile(i, nct), 0)),
                  pl.BlockSpec((tm, D_MODEL), lambda i: (_lat_tile(i, nct), 0)),
                  mod_row(gate_chunk),
                  pl.BlockSpec((1, D_MODEL), lambda i: (0, 0)),
                  mod_row(sc_chunk), mod_row(sh_chunk)],
        out_specs=[pl.BlockSpec((tm, D_MODEL), lambda i: (i, 0)),
                   pl.BlockSpec((D_MODEL, tm), lambda i: (0, i))],
        out_shape=[jax.ShapeDtypeStruct((t, D_MODEL), F32),
                   jax.ShapeDtypeStruct((D_MODEL, t), BF16)],
        compiler_params=_cp(("parallel",), vmem_mb=56),
        name="out_proj",
    )(a, w, *x_parts, mod3, norm_g, mod3, mod3)


def _peer_candidate_blocks():
    kk = PEER_TOPK
    blocks = [(a, a + 1, kk // (a + 1)) for a in range(kk // 2)]
    blocks.append((kk // 2, kk, 1))
    return blocks


def _peer_route_kernel(h2t_ref, wqt_ref, keys_ref, s1_ref, e1_ref, s2_ref, e2_ref, qt_s, sv_s, cand_s):
    tt = h2t_ref.shape[1]
    kk = PEER_TOPK
    lanes = 128
    qt_s[...] = jnp.dot(wqt_ref[...], h2t_ref[...], preferred_element_type=F32).astype(BF16)

    def score_step(h, carry):
        for half, s_ref in ((0, s1_ref), (1, s2_ref)):
            c = 2 * h + half
            r0 = pl.multiple_of(c * PEER_HALF, PEER_HALF)
            s_ref[h] = jnp.dot(keys_ref[c], qt_s[pl.ds(r0, PEER_HALF), :], preferred_element_type=F32)
            for lt in range(tt // lanes):
                ln = slice(lt * lanes, (lt + 1) * lanes)
                cur = s_ref[h, :, ln]
                for r in range(kk):
                    m = jnp.max(cur, axis=0, keepdims=True)
                    sv_s[c, r:r + 1, ln] = m
                    cur = jnp.where(cur == m, -jnp.inf, cur)
        return carry

    lax.fori_loop(0, PEER_HEADS, score_step, 0)

    blocks = _peer_candidate_blocks()
    sub = lax.broadcasted_iota(jnp.int32, (8, lanes), 0)

    def head_step(h, carry):
        for lt in range(tt // lanes):
            ln = slice(lt * lanes, (lt + 1) * lanes)
            sv1 = sv_s[2 * h, :, ln]
            sv2 = sv_s[2 * h + 1, :, ln]
            row = 0
            for a_lo, a_hi, n_b in blocks:
                if a_hi - a_lo == 1:
                    n_rows = -(-n_b // 8) * 8
                    blk = sv1[a_lo:a_lo + 1, :] + sv2[0:n_rows, :]
                    if n_b < n_rows:
                        blk = jnp.where(sub < n_b, blk, -jnp.inf)
                else:
                    n_rows = a_hi - a_lo
                    blk = sv1[a_lo:a_hi, :] + sv2[0:1, :]
                cand_s[row:row + n_rows, ln] = blk
                row += n_rows
            cur = cand_s[:, ln]
            m0 = sv1[0:1, :] + sv2[0:1, :]
            taken = jnp.zeros((1, lanes), F32)
            tau = m0
            zsum = jnp.zeros((1, lanes), F32)
            for r in range(kk):
                m = jnp.max(cur, axis=0, keepdims=True)
                eq = cur == m
                cnt = jnp.sum(jnp.where(eq, 1.0, 0.0), axis=0, keepdims=True)
                active = taken < kk
                tau = jnp.where(active, m, tau)
                zsum = zsum + jnp.where(active, cnt * jnp.exp(m - m0), 0.0)
                taken = taken + jnp.where(active, cnt, 0.0)
                cur = jnp.where(eq, -jnp.inf, cur)
            s1 = s1_ref[h, :, ln]
            th1 = jnp.full(s1.shape, jnp.inf, F32)
            for b in range(kk):
                th1 = jnp.where(s1 + sv2[b:b + 1, :] >= tau, sv2[b:b + 1, :], th1)
            s1_ref[h, :, ln] = th1
            e1_ref[h, :, ln] = jnp.exp(s1 - sv1[0:1, :]) / zsum
            e2_ref[h, :, ln] = jnp.exp(s2_ref[h, :, ln] - sv2[0:1, :])
        return carry

    lax.fori_loop(0, PEER_HEADS, head_step, 0)


def _peer_route(h2t, wqt, keys):
    t = h2t.shape[1]
    tt = 512
    n_cand_rows = sum((-(-n_b // 8) * 8) if a_hi - a_lo == 1 else a_hi - a_lo
                      for a_lo, a_hi, n_b in _peer_candidate_blocks())
    slab = pl.BlockSpec((PEER_HEADS, PEER_N_KEYS, tt), lambda i: (0, 0, i))
    slab_shape = jax.ShapeDtypeStruct((PEER_HEADS, PEER_N_KEYS, t), F32)
    return pl.pallas_call(
        _peer_route_kernel,
        grid=(t // tt,),
        in_specs=[pl.BlockSpec((D_MODEL, tt), lambda i: (0, i)),
                  pl.BlockSpec(wqt.shape, lambda i: (0, 0)),
                  pl.BlockSpec(keys.shape, lambda i: (0, 0, 0))],
        out_specs=[slab, slab, slab, slab],
        out_shape=[slab_shape, slab_shape, slab_shape, slab_shape],
        scratch_shapes=[pltpu.VMEM((wqt.shape[0], tt), BF16),
                        pltpu.VMEM((2 * PEER_HEADS, PEER_TOPK, tt), F32),
                        pltpu.VMEM((n_cand_rows, tt), F32)],
        compiler_params=_cp(("parallel",)),
        name="peer_route",
    )(h2t, wqt, keys)


def _gelu_exact(x):
    return 0.5 * x * (1.0 + lax.erf(x * (2.0 ** -0.5)))


PEER_ROWS_PER_TILE = 8
PEER_J_CHUNK = 32
PEER_ROW_GROUP = 8
PEER_ACT_SLAB = 32
PEER_K_SLICE = 256


def _peer_dense_kernel(h2t_ref, u_ref, vt_ref, th1_ref, e1_ref, s2_ref, e2_ref, o_ref, g_s, pre_s, act_s):
    @pl.when(pl.program_id(1) == 0)
    def _():
        o_ref[...] = jnp.zeros_like(o_ref)

    tt = h2t_ref.shape[1]
    jc = PEER_J_CHUNK
    n_k = h2t_ref.shape[0] // PEER_K_SLICE
    chunks_per_k = (tt // 128) * (PEER_N_KEYS // jc) // n_k
    pace = []
    chunk = 0
    for lt in range(tt // 128):
        ln = slice(lt * 128, (lt + 1) * 128)
        for j0 in range(0, PEER_N_KEYS, jc):
            for i0 in range(0, PEER_ROWS_PER_TILE, PEER_ROW_GROUP):
                rows = range(i0, i0 + PEER_ROW_GROUP)
                acc = {ii: None for ii in rows}
                for h in range(PEER_HEADS):
                    s2 = s2_ref[h, j0:j0 + jc, ln]
                    e2 = e2_ref[h, j0:j0 + jc, ln]
                    for ii in rows:
                        term = jnp.where(s2 >= th1_ref[h, ii:ii + 1, ln], e2, 0.0) * e1_ref[h, ii:ii + 1, ln]
                        acc[ii] = term if acc[ii] is None else acc[ii] + term
                for ii in rows:
                    sl = slice(ii * PEER_N_KEYS + j0, ii * PEER_N_KEYS + j0 + jc)
                    g_s[sl, ln] = acc[ii]
            if chunk % chunks_per_k == 0:
                whole = functools.reduce(jnp.add, [acc[ii] for ii in rows])
                pace.append(jnp.minimum(jnp.sum(whole, axis=0, keepdims=True), 0.0))
            chunk += 1

    pre = None
    for k in range(n_k):
        ks = slice(k * PEER_K_SLICE, (k + 1) * PEER_K_SLICE)
        zero = jnp.concatenate([pace[k]] * (tt // 128), axis=1).astype(BF16)
        part = jnp.dot(u_ref[:, ks], h2t_ref[ks, :] + zero, preferred_element_type=F32)
        pre = part if pre is None else pre + part
    pre_s[...] = pre
    for r0 in range(0, g_s.shape[0], PEER_ACT_SLAB):
        sl = slice(r0, r0 + PEER_ACT_SLAB)
        act_s[sl, :] = (g_s[sl, :] * _gelu_exact(pre_s[sl, :])).astype(BF16)
    o_ref[...] += jnp.dot(vt_ref[...], act_s[...], preferred_element_type=F32)


def _peer_dense(h2t, u, vt, th1, e1, s2, e2):
    t = h2t.shape[1]
    n_exp = u.shape[0]
    tt = 512
    te = PEER_ROWS_PER_TILE * PEER_N_KEYS
    rows = pl.BlockSpec((PEER_HEADS, PEER_ROWS_PER_TILE, tt), lambda i, j: (0, j, i))
    slab = pl.BlockSpec((PEER_HEADS, PEER_N_KEYS, tt), lambda i, j: (0, 0, i))
    return pl.pallas_call(
        _peer_dense_kernel,
        grid=(t // tt, n_exp // te),
        in_specs=[pl.BlockSpec((D_MODEL, tt), lambda i, j: (0, i)),
                  pl.BlockSpec((te, D_MODEL), lambda i, j: (j, 0)),
                  pl.BlockSpec((D_MODEL, te), lambda i, j: (0, j)),
                  rows, rows, slab, slab],
        out_specs=pl.BlockSpec((D_MODEL, tt), lambda i, j: (0, i)),
        out_shape=jax.ShapeDtypeStruct((D_MODEL, t), F32),
        scratch_shapes=[pltpu.VMEM((te, tt), F32), pltpu.VMEM((te, tt), F32), pltpu.VMEM((te, tt), BF16)],
        compiler_params=_cp(("parallel", "arbitrary"), vmem_mb=56),
        name="peer_dense",
    )(h2t, u, vt, th1, e1, s2, e2)


def _final_kernel(x_ref, pt_ref, g2_ref, fg_ref, oc_ref, ol_ref, *, n_ctx_tiles):
    x = x_ref[...] + g2_ref[...] * pt_ref[...].T
    y = x * lax.rsqrt(jnp.mean(x * x, axis=-1, keepdims=True) + RMS_EPS) * fg_ref[...]
    ol_ref[...] = y

    @pl.when(pl.program_id(0) < n_ctx_tiles)
    def _():
        oc_ref[...] = y


def _final(x, p_t, mod3, gate_chunk, final_g, n_ctx_rows):
    t = x.shape[0]
    tm = 512
    nct = n_ctx_rows // tm
    row = functools.partial(_mod_row, tm=tm, n_ctx_rows=n_ctx_rows)
    return pl.pallas_call(
        functools.partial(_final_kernel, n_ctx_tiles=nct),
        grid=(t // tm,),
        in_specs=[pl.BlockSpec((tm, D_MODEL), lambda i: (i, 0)),
                  pl.BlockSpec((D_MODEL, tm), lambda i: (0, i)),
                  pl.BlockSpec((None, 1, D_MODEL), lambda i: (row(i), 0, gate_chunk)),
                  pl.BlockSpec((1, D_MODEL), lambda i: (0, 0))],
        out_specs=[pl.BlockSpec((tm, D_MODEL), lambda i: (_ctx_tile(i, nct), 0)),
                   pl.BlockSpec((tm, D_MODEL), lambda i: (_lat_tile(i, nct), 0))],
        out_shape=[jax.ShapeDtypeStruct((n_ctx_rows, D_MODEL), F32),
                   jax.ShapeDtypeStruct((t - n_ctx_rows, D_MODEL), F32)],
        compiler_params=_cp(("arbitrary",)),
        name="final_norm",
    )(x, p_t, mod3, final_g)


def _group_heads(p):
    return jnp.transpose(p.reshape(2, SSD_GROUPS, HEADS_PER_GROUP), (1, 0, 2)).reshape(
        SSD_GROUPS, 2 * HEADS_PER_GROUP)


def _layer(x_parts, lp, mod3, n_ctx_batch, n_lat_batch, cache):
    n_ctx_rows = n_ctx_batch * SEQ
    t = n_ctx_rows + n_lat_batch * DEC_SEQ
    w_in = lp['w_in']
    na_w = NA_HEADS * NA_HEAD_DIM
    o_z = 3 * na_w
    o_xbc = o_z + SSD_D_INNER
    o_dt = o_xbc + SSD_CONV_CH
    o_g = o_dt + 2 * SSD_HEADS

    h1 = _norm_mod(x_parts, lp['norm1_g'][None, :], mod3, 1, 0, n_ctx_rows)
    q = _mm(h1, w_in, 0, na_w, BF16, name="in_q")
    k_new, k_lat = _mm_split(h1, w_in, na_w, na_w, F32, n_ctx_rows, name="in_k")
    v_new, v_lat = _mm_split(h1, w_in, 2 * na_w, na_w, F32, n_ctx_rows, name="in_v")
    z = _mm(h1, w_in, o_z, SSD_D_INNER, F32, name="in_z")
    xbc = _mm(h1, w_in, o_xbc, SSD_CONV_CH, F32, name="in_xbc")
    dt_raw = _mm(h1, w_in, o_dt, 2 * SSD_HEADS, F32, name="in_dt")
    gates = _mm(h1, w_in[:, o_g:], 0, 2 * D_MODEL, F32, name="in_gates")

    k_ctx, v_ctx, init_f, init_b = cache
    na_ctx = _ctx_attention(q, k_new, v_new, n_ctx_batch)
    na_lat = _na_latent(q, k_lat, v_lat, k_ctx.reshape(n_lat_batch, -1, na_w),
                        v_ctx.reshape(n_lat_batch, -1, na_w), lp['na_rpb'], n_ctx_rows, n_lat_batch)

    dtg = jnp.transpose(dt_raw.reshape(t, 2, SSD_GROUPS, HEADS_PER_GROUP), (2, 0, 1, 3)).reshape(
        SSD_GROUPS, t, 2 * HEADS_PER_GROUP)
    dtgt = jnp.transpose(dtg, (0, 2, 1))
    conv_w8 = jnp.concatenate([lp['conv_w'], jnp.zeros((8 - SSD_CONV, SSD_CONV_CH), F32)], axis=0)
    dtb = _group_heads(lp['ssd_dt_bias'])
    alog = _group_heads(lp['ssd_a_log'])
    dexp = jnp.repeat(lp['ssd_d'], SSD_HEAD_DIM)[None, :]
    ssd_args = (xbc, z, dtg, dtgt, conv_w8, lp['conv_b'][None, :], dtb[:, None, :], dtb[:, :, None],
                alog[:, None, :], alog[:, :, None], dexp, lp['ssd_norm_g'][None, :])
    ssd_ctx, s_f, s_b = _ssd(*ssd_args, row_start=0, nb=n_ctx_batch, seq=SEQ)
    ssd_lat = _ssd(*ssd_args, row_start=n_ctx_rows, nb=n_lat_batch, seq=DEC_SEQ, init=(init_f, init_b))

    mixed = _mix((na_ctx, na_lat), (ssd_ctx, ssd_lat), lp['w_na_proj'].astype(BF16),
                 lp['w_ssd_proj'].astype(BF16), gates, n_ctx_rows)
    x1, h2t = _outproj_residual_norm(mixed, lp['w_out'].astype(BF16), x_parts, mod3, 2,
                                     lp['norm2_g'][None, :], 4, 3, n_ctx_rows)
    keys = lp['peer_keys'].reshape(2 * PEER_HEADS, PEER_N_KEYS, PEER_HALF).astype(BF16)
    th1, e1, s2, e2 = _peer_route(h2t, lp['peer_wq'].T.astype(BF16), keys)
    peer_t = _peer_dense(h2t, lp['peer_u'].astype(BF16), lp['peer_v'].T.astype(BF16), th1, e1, s2, e2)
    new_k = k_new.reshape(n_ctx_batch, SEQ, NA_HEADS, NA_HEAD_DIM)
    new_v = v_new.reshape(n_ctx_batch, SEQ, NA_HEADS, NA_HEAD_DIM)
    return x1, peer_t, (new_k, new_v, s_f, s_b)


def kernel(x_prompt, x_sample, c, c_ctx, cache_na_k, cache_na_v, state_ssd_fwd, state_ssd_bwd, ada_w, ada_b, norm1_g, norm2_g, w_in, conv_w, conv_b, na_rpb, ssd_a_log, ssd_dt_bias, ssd_d, ssd_norm_g, w_na_proj, w_ssd_proj, w_out, peer_wq, peer_keys, peer_u, peer_v, final_g):
    n_ctx_batch = x_prompt.shape[0]
    n_lat_batch = x_sample.shape[0]
    depth = ada_w.shape[0]
    n_ctx_rows = n_ctx_batch * SEQ
    assert x_prompt.shape[1] == SEQ and x_sample.shape[1] == DEC_SEQ
    assert 1 + n_lat_batch <= MOD_ROWS and n_ctx_rows % DEC_SEQ == 0
    assert depth == 1, "the final-norm kernel folds in the PEER residual of the single trunk layer"

    x_parts = (x_prompt.reshape(-1, D_MODEL), x_sample.reshape(-1, D_MODEL))
    cvec = jnp.concatenate([c_ctx[None, :], c, jnp.zeros((MOD_ROWS - 1 - n_lat_batch, D_MODEL), F32)], axis=0)
    lp = {'norm1_g': norm1_g[0], 'norm2_g': norm2_g[0], 'w_in': w_in[0], 'conv_w': conv_w[0],
          'conv_b': conv_b[0], 'na_rpb': na_rpb[0], 'ssd_a_log': ssd_a_log[0],
          'ssd_dt_bias': ssd_dt_bias[0], 'ssd_d': ssd_d[0], 'ssd_norm_g': ssd_norm_g[0],
          'w_na_proj': w_na_proj[0], 'w_ssd_proj': w_ssd_proj[0], 'w_out': w_out[0],
          'peer_wq': peer_wq[0], 'peer_keys': peer_keys[0], 'peer_u': peer_u[0], 'peer_v': peer_v[0]}
    mod3 = _ada(cvec, ada_w[0], ada_b[0][None, :])[:, None, :]
    cache = (cache_na_k[:, 0], cache_na_v[:, 0], state_ssd_fwd[:, 0], state_ssd_bwd[:, 0])
    x1, peer_t, (new_k, new_v, new_f, new_b) = _layer(x_parts, lp, mod3, n_ctx_batch, n_lat_batch, cache)
    y_ctx, y_lat = _final(x1, peer_t, mod3, 5, final_g[None, :], n_ctx_rows)
    return (y_ctx.reshape(x_prompt.shape), y_lat.reshape(x_sample.shape),
            new_k[:, None], new_v[:, None], new_f[:, None], new_b[:, None])
```
